```python
import jax, jax.numpy as jnp
from jax import lax
import numpy as np

D_MODEL = 2048
BATCH = 32
SEQ = 256
DEPTH = 4
DEC_BATCH = 2
DEC_SEQ = 2048
PAST_LEN = 256

GRID_W = 64
HEAD_DIM = 128
N_Q_HEADS = 12
N_KV_HEADS = 4
Q_PER_KV = N_Q_HEADS // N_KV_HEADS
ATTN_WIDTH = N_Q_HEADS * HEAD_DIM
KV_WIDTH = N_KV_HEADS * HEAD_DIM
WINDOW = 128
BLOCK = 128
ROPE_THETA = 10000.0
FOURIER_GROUPS = 4
FOURIER_WIDTH = D_MODEL // 4
FOURIER_GROUP_DIM = FOURIER_WIDTH // FOURIER_GROUPS
POOL_WINDOWS = (2, 4, 8, 16)
POOL_WIDTH = D_MODEL // 2
POOL_GROUP_DIM = POOL_WIDTH // len(POOL_WINDOWS)
SGU_HEADS = 4
SGU_WIDTH = D_MODEL // 2
SGU_GROUP_DIM = SGU_WIDTH // SGU_HEADS
SGU_CHUNK = 128
MIX_WIDTH = ATTN_WIDTH + FOURIER_WIDTH
IN_WIDTH = 2 * ATTN_WIDTH + 2 * KV_WIDTH + 2 * FOURIER_WIDTH
EVEN_SPLITS = (ATTN_WIDTH, ATTN_WIDTH + KV_WIDTH, ATTN_WIDTH + 2 * KV_WIDTH,
               2 * ATTN_WIDTH + 2 * KV_WIDTH, 2 * ATTN_WIDTH + 2 * KV_WIDTH + FOURIER_WIDTH)
ODD_SPLITS = (POOL_WIDTH, 2 * POOL_WIDTH, 2 * POOL_WIDTH + SGU_WIDTH, 2 * POOL_WIDTH + 2 * SGU_WIDTH)
N_ATTN_LAYERS = (DEPTH + 1) // 2
N_POOL_LAYERS = DEPTH // 2
EPS = 1e-6
NEG_INF = -1e30

kernel_name = "hybrid_diffusion_prefix_step"


def rmsnorm(x, g):
    x32 = x.astype(jnp.float32)
    y = x32 * lax.rsqrt(jnp.mean(x32 * x32, axis=-1, keepdims=True) + EPS)
    return (y * g.astype(jnp.float32)).astype(x.dtype)


def axial_rope_tables(t_len):
    rows = t_len // GRID_W
    row = jnp.repeat(jnp.arange(rows), GRID_W).astype(jnp.float32)
    col = jnp.tile(jnp.arange(GRID_W), rows).astype(jnp.float32)
    n_freq = HEAD_DIM // 4
    inv = ROPE_THETA ** (-jnp.arange(n_freq, dtype=jnp.float32) / n_freq)
    ang = jnp.stack([row[:, None] * inv, col[:, None] * inv], axis=1)
    return jnp.cos(ang), jnp.sin(ang)


def apply_axial_rope(x, cos, sin):
    shp = x.shape
    xr = x.reshape(shp[:-1] + (2, 2, HEAD_DIM // 4)).astype(jnp.float32)
    c = cos[None, :, None]
    s = sin[None, :, None]
    x1 = xr[..., 0, :]
    x2 = xr[..., 1, :]
    out = jnp.stack([x1 * c - x2 * s, x2 * c + x1 * s], axis=-2)
    return out.reshape(shp).astype(x.dtype)


def context_attention(q, k, v, sink):
    B, L = q.shape[:2]
    s = jnp.einsum('blhgd,bmhd->bhglm', q, k, preferred_element_type=jnp.float32) * (HEAD_DIM ** -0.5)
    sink_l = jnp.broadcast_to(sink.astype(jnp.float32)[None, :, :, None, None], s.shape[:-1] + (1,))
    p = jax.nn.softmax(jnp.concatenate([s, sink_l], axis=-1), axis=-1)[..., :-1]
    o = jnp.einsum('bhglm,bmhd->blhgd', p.astype(v.dtype), v)
    return o.reshape(B, L, ATTN_WIDTH)


def latent_window_attention(q, k, v, k_ctx, v_ctx, sink):
    B, T = q.shape[:2]
    nb = T // BLOCK
    qb = q.reshape(B, nb, BLOCK, N_KV_HEADS, Q_PER_KV, HEAD_DIM)
    pad = ((0, 0), (BLOCK, BLOCK), (0, 0), (0, 0))
    kb = jnp.pad(k, pad).reshape(B, nb + 2, BLOCK, N_KV_HEADS, HEAD_DIM)
    vb = jnp.pad(v, pad).reshape(B, nb + 2, BLOCK, N_KV_HEADS, HEAD_DIM)
    kw = jnp.concatenate([kb[:, :-2], kb[:, 1:-1], kb[:, 2:]], axis=2)
    vw = jnp.concatenate([vb[:, :-2], vb[:, 1:-1], vb[:, 2:]], axis=2)
    scale = HEAD_DIM ** -0.5
    s_loc = jnp.einsum('bnqhgd,bnkhd->bnhgqk', qb, kw, preferred_element_type=jnp.float32) * scale
    blk = jnp.arange(nb)[:, None]
    qpos = blk * BLOCK + jnp.arange(BLOCK)[None, :]
    kpos = (blk - 1) * BLOCK + jnp.arange(3 * BLOCK)[None, :]
    valid = ((jnp.abs(qpos[:, :, None] - kpos[:, None, :]) <= WINDOW)
             & (kpos >= 0)[:, None, :] & (kpos < T)[:, None, :])
    s_loc = jnp.where(valid[None, :, None, None], s_loc, NEG_INF)
    s_ctx = jnp.einsum('bnqhgd,blhd->bnhgql', qb, k_ctx, preferred_element_type=jnp.float32) * scale
    sink_l = jnp.broadcast_to(sink.astype(jnp.float32)[None, None, :, :, None, None], s_loc.shape[:-1] + (1,))
    p = jax.nn.softmax(jnp.concatenate([s_loc, s_ctx, sink_l], axis=-1), axis=-1)
    p_loc = p[..., :3 * BLOCK].astype(v.dtype)
    p_ctx = p[..., 3 * BLOCK:-1].astype(v.dtype)
    o = (jnp.einsum('bnhgqk,bnkhd->bnqhgd', p_loc, vw)
         + jnp.einsum('bnhgql,blhd->bnqhgd', p_ctx, v_ctx))
    return o.reshape(B, T, ATTN_WIDTH)


def fourier_mix(u):
    B, T, _ = u.shape
    ug = u.reshape(B, T, FOURIER_GROUPS, FOURIER_GROUP_DIM).astype(jnp.float32)
    f = jnp.fft.fft2(ug, axes=(1, 3), norm="ortho").real
    return f.reshape(B, T, FOURIER_WIDTH).astype(u.dtype)


def multiscale_pool(u, pool_w, pool_scale):
    B, T, _ = u.shape
    ug = u.reshape(B, T, len(POOL_WINDOWS), POOL_GROUP_DIM).astype(jnp.float32)
    csum = jnp.pad(jnp.cumsum(ug, axis=1), ((0, 0), (1, 0), (0, 0), (0, 0)))
    pos = jnp.arange(T)
    outs = []
    for g, w in enumerate(POOL_WINDOWS):
        lo = jnp.clip(pos - w // 2, 0, T)
        hi = jnp.clip(pos + w // 2, 0, T)
        cg = csum[:, :, g]
        cnt = (hi - lo).astype(jnp.float32)[None, :, None]
        outs.append((cg[:, hi] - cg[:, lo]) / cnt - ug[:, :, g])
    pooled = jnp.stack(outs, axis=2)
    mixed = jnp.einsum('btgc,gcd->btgd', pooled, pool_w.astype(jnp.float32))
    return (mixed.reshape(B, T, POOL_WIDTH) * pool_scale.astype(jnp.float32)).astype(u.dtype)


def spatial_gating(u, v, sgu_norm, sgu_w, sgu_b):
    B, T, _ = u.shape
    nc = T // SGU_CHUNK
    v32 = v.reshape(B, T, SGU_HEADS, SGU_GROUP_DIM).astype(jnp.float32)
    mu = jnp.mean(v32, axis=-1, keepdims=True)
    var = jnp.mean(jnp.square(v32 - mu), axis=-1, keepdims=True)
    vn = ((v32 - mu) * lax.rsqrt(var + EPS) * sgu_norm.astype(jnp.float32).reshape(SGU_HEADS, SGU_GROUP_DIM)).astype(v.dtype)
    vc = vn.reshape(B, nc, SGU_CHUNK, SGU_HEADS, SGU_GROUP_DIM)
    mixed = jnp.einsum('hpq,bnqhc->bnphc', sgu_w, vc) + sgu_b.T[None, None, :, :, None]
    return u * mixed.reshape(B, T, SGU_WIDTH)


def trunk_layer(l, x, cond, w_ada, b_ada, norm_pre, norm_post, w_in, w_out,
                attn_sink, pool_w, pool_scale, sgu_norm, sgu_w, sgu_b,
                rope, ctx_k, ctx_v):
    B, T, _ = x.shape
    j = l // 2
    mod = jax.nn.silu(cond) @ w_ada[l] + b_ada[l]
    shift, scale, gate = jnp.split(mod[:, None, :], 3, axis=-1)
    h = rmsnorm(x, norm_pre[l]) * (1 + scale) + shift
    proj = h @ w_in[l]
    k = v = None
    if l % 2 == 0:
        q, k, v, g_a, u_b, g_b = jnp.split(proj, EVEN_SPLITS, axis=-1)
        q = q.reshape(B, T, N_Q_HEADS, HEAD_DIM)
        k = k.reshape(B, T, N_KV_HEADS, HEAD_DIM)
        v = v.reshape(B, T, N_KV_HEADS, HEAD_DIM)
        sink = attn_sink[j].reshape(N_KV_HEADS, Q_PER_KV)
        if ctx_k is None:
            o_a = context_attention(q.reshape(B, T, N_KV_HEADS, Q_PER_KV, HEAD_DIM), k, v, sink)
        else:
            cos, sin = rope
            qr = apply_axial_rope(q, cos, sin).reshape(B, T, N_KV_HEADS, Q_PER_KV, HEAD_DIM)
            kr = apply_axial_rope(k, cos, sin)
            o_a = latent_window_attention(qr, kr, v, ctx_k, ctx_v, sink)
        o = jnp.concatenate([jax.nn.silu(g_a) * o_a, jax.nn.silu(g_b) * fourier_mix(u_b)], axis=-1)
    else:
        u_c, g_c, u_d, v_d, g_d = jnp.split(proj, ODD_SPLITS, axis=-1)
        o = jnp.concatenate([
            jax.nn.silu(g_c) * multiscale_pool(u_c, pool_w[j], pool_scale[j]),
            jax.nn.silu(g_d) * spatial_gating(u_d, v_d, sgu_norm[j], sgu_w[j], sgu_b[j])], axis=-1)
    x = x + gate * rmsnorm(o @ w_out[l], norm_post[l])
    return x, k, v


def setup_inputs(seed: int = 0) -> dict:
    key = jax.random.key(seed)
    ks = jax.random.split(key, 18)
    f32 = jnp.float32
    nrm = lambda k, shp: jax.random.normal(k, shp, f32)
    d_inv = D_MODEL ** -0.5
    return {
        "x_prompt": nrm(ks[0], (BATCH, SEQ, D_MODEL)),
        "x_sample": nrm(ks[1], (DEC_BATCH, DEC_SEQ, D_MODEL)),
        "cache_k": nrm(ks[2], (DEC_BATCH, N_ATTN_LAYERS, PAST_LEN, N_KV_HEADS, HEAD_DIM)),
        "cache_v": nrm(ks[3], (DEC_BATCH, N_ATTN_LAYERS, PAST_LEN, N_KV_HEADS, HEAD_DIM)),
        "c": nrm(ks[4], (DEC_BATCH, D_MODEL)),
        "c_ctx": nrm(ks[5], (D_MODEL,)),
        "w_ada": nrm(ks[6], (DEPTH, D_MODEL, 3 * D_MODEL)) * (0.5 * d_inv),
        "b_ada": nrm(ks[7], (DEPTH, 3 * D_MODEL)) * 0.02,
        "norm_pre": 1.0 + 0.02 * nrm(ks[8], (DEPTH, D_MODEL)),
        "norm_post": 1.0 + 0.02 * nrm(ks[9], (DEPTH, D_MODEL)),
        "w_in": nrm(ks[10], (DEPTH, D_MODEL, IN_WIDTH)) * d_inv,
        "w_out": nrm(ks[11], (DEPTH, MIX_WIDTH, D_MODEL)) * (MIX_WIDTH ** -0.5),
        "attn_sink": 0.5 * nrm(ks[12], (N_ATTN_LAYERS, N_Q_HEADS)),
        "pool_w": nrm(ks[13], (N_POOL_LAYERS, len(POOL_WINDOWS), POOL_GROUP_DIM, POOL_GROUP_DIM)) * (POOL_GROUP_DIM ** -0.5),
        "pool_scale": 1.0 + 0.02 * nrm(ks[14], (N_POOL_LAYERS, POOL_WIDTH)),
        "sgu_norm": 1.0 + 0.02 * nrm(ks[15], (N_POOL_LAYERS, SGU_WIDTH)),
        "sgu_w": nrm(ks[16], (N_POOL_LAYERS, SGU_HEADS, SGU_CHUNK, SGU_CHUNK)) * (SGU_CHUNK ** -0.5),
        "sgu_b": 1.0 + 0.02 * nrm(ks[17], (N_POOL_LAYERS, SGU_HEADS, SGU_CHUNK)),
    }


def reference(x_prompt, x_sample, cache_k, cache_v, c, c_ctx, w_ada, b_ada, norm_pre, norm_post,
              w_in, w_out, attn_sink, pool_w, pool_scale, sgu_norm, sgu_w, sgu_b):
    rope = axial_rope_tables(x_sample.shape[1])
    cond_ctx = c_ctx[None, :]
    xp = x_prompt
    xs = x_sample
    new_k = []
    new_v = []
    for l in range(DEPTH):
        j = l // 2
        xp, kp, vp = trunk_layer(l, xp, cond_ctx, w_ada, b_ada, norm_pre, norm_post, w_in, w_out,
                                 attn_sink, pool_w, pool_scale, sgu_norm, sgu_w, sgu_b,
                                 None, None, None)
        if l % 2 == 0:
            new_k.append(kp)
            new_v.append(vp)
            ck, cv = cache_k[:, j], cache_v[:, j]
        else:
            ck, cv = None, None
        xs, _, _ = trunk_layer(l, xs, c, w_ada, b_ada, norm_pre, norm_post, w_in, w_out,
                               attn_sink, pool_w, pool_scale, sgu_norm, sgu_w, sgu_b,
                               rope, ck, cv)
    new_cache_k = jnp.stack(new_k, axis=1)
    new_cache_v = jnp.stack(new_v, axis=1)
    return (xp, xs, new_cache_k, new_cache_v)
```

```python
import functools

import numpy as np
import jax
import jax.numpy as jnp
from jax import lax
from jax.experimental import pallas as pl
from jax.experimental.pallas import tpu as pltpu

F32 = jnp.float32
BF16 = jnp.bfloat16

D_MODEL = 2048
DEPTH = 4
GRID_W = 64
HEAD_DIM = 128
N_Q_HEADS = 12
N_KV_HEADS = 4
Q_PER_KV = N_Q_HEADS // N_KV_HEADS
ATTN_WIDTH = N_Q_HEADS * HEAD_DIM
KV_WIDTH = N_KV_HEADS * HEAD_DIM
WINDOW = 128
BLOCK = 128
ROPE_THETA = 10000.0
FOURIER_GROUPS = 4
FOURIER_WIDTH = D_MODEL // 4
FOURIER_GROUP_DIM = FOURIER_WIDTH // FOURIER_GROUPS
POOL_WINDOWS = (2, 4, 8, 16)
POOL_WIDTH = D_MODEL // 2
POOL_GROUP_DIM = POOL_WIDTH // len(POOL_WINDOWS)
SGU_HEADS = 4
SGU_WIDTH = D_MODEL // 2
SGU_GROUP_DIM = SGU_WIDTH // SGU_HEADS
SGU_CHUNK = 128
IN_WIDTH = 2 * ATTN_WIDTH + 2 * KV_WIDTH + 2 * FOURIER_WIDTH
EPS = 1e-6
NEG_INF = -1e30

Q_OFF, K_OFF, V_OFF = 0, ATTN_WIDTH, ATTN_WIDTH + KV_WIDTH
GA_OFF = ATTN_WIDTH + 2 * KV_WIDTH
UB_OFF = GA_OFF + ATTN_WIDTH
GB_OFF = UB_OFF + FOURIER_WIDTH
UC_OFF, GC_OFF, UD_OFF, VD_OFF, GD_OFF = (i * POOL_WIDTH for i in range(5))

V7X_VMEM_BYTES = 64 * 1024 * 1024
VMEM_LIMIT = V7X_VMEM_BYTES - 8 * 1024 * 1024

TOKEN_TILE = 512
PROJ_CHUNK = 512
SEQ_TILE = 256
ADA_TILE = 1024
COND_ROWS = 8
FOURIER_ROW_TILE = 512


def _silu(x):
    return x / (1.0 + jnp.exp(-x))


def _params(*sem):
    return pltpu.CompilerParams(dimension_semantics=sem, vmem_limit_bytes=VMEM_LIMIT)


def _dot(a, b):
    return jnp.dot(a, b, preferred_element_type=F32)


def _dot_nt(a, b):
    return lax.dot_general(a, b, (((1,), (1,)), ((), ())), preferred_element_type=F32)


def _ada_kernel(c_ref, w_ref, b_ref, o_ref):
    s = _silu(c_ref[...]).astype(BF16)
    o_ref[...] = _dot(s, w_ref[...].astype(BF16)) + b_ref[...]


def _ada(conds, w_ada, b_ada):
    n = w_ada.shape[-1]
    return pl.pallas_call(
        _ada_kernel,
        grid=(DEPTH, n // ADA_TILE),
        in_specs=[
            pl.BlockSpec((COND_ROWS, D_MODEL), lambda l, j: (0, 0)),
            pl.BlockSpec((None, D_MODEL, ADA_TILE), lambda l, j: (l, 0, j)),
            pl.BlockSpec((None, 1, ADA_TILE), lambda l, j: (l, 0, j)),
        ],
        out_specs=pl.BlockSpec((None, COND_ROWS, ADA_TILE), lambda l, j: (l, 0, j)),
        out_shape=jax.ShapeDtypeStruct((DEPTH, COND_ROWS, n), F32),
        compiler_params=_params("arbitrary", "arbitrary"),
        name="ada_modulation",
    )(conds, w_ada, b_ada.reshape(DEPTH, 1, n))


def _inproj_kernel(x_ref, mod_ref, g_ref, w_ref, proj_ref, *kv_refs):
    x = x_ref[...]
    shift = mod_ref[:, 0:D_MODEL]
    scale = mod_ref[:, D_MODEL:2 * D_MODEL]
    ms = jnp.mean(x * x, axis=-1, keepdims=True)
    h = (x * lax.rsqrt(ms + EPS)) * g_ref[...]
    h = (h * (1.0 + scale) + shift).astype(BF16)
    for c0 in range(0, IN_WIDTH, PROJ_CHUNK):
        acc = _dot(h, w_ref[:, c0:c0 + PROJ_CHUNK])
        proj_ref[:, c0:c0 + PROJ_CHUNK] = acc.astype(BF16)
        if kv_refs:
            if c0 == K_OFF:
                kv_refs[0][...] = acc
            elif c0 == V_OFF:
                kv_refs[1][...] = acc


def _inproj(x, mod, norm_pre, w_in_bf, layer, emit_kv):
    assert KV_WIDTH == PROJ_CHUNK and K_OFF % PROJ_CHUNK == 0
    m = x.shape[0]
    n_seq = mod.shape[1]
    tiles_per_seq = m // n_seq // TOKEN_TILE
    out_shape = [jax.ShapeDtypeStruct((m, IN_WIDTH), BF16)]
    out_specs = [pl.BlockSpec((TOKEN_TILE, IN_WIDTH), lambda i: (i, 0))]
    if emit_kv:
        out_shape += [jax.ShapeDtypeStruct((m, KV_WIDTH), F32)] * 2
        out_specs += [pl.BlockSpec((TOKEN_TILE, KV_WIDTH), lambda i: (i, 0))] * 2
    return pl.pallas_call(
        _inproj_kernel,
        grid=(m // TOKEN_TILE,),
        in_specs=[
            pl.BlockSpec((TOKEN_TILE, D_MODEL), lambda i: (i, 0)),
            pl.BlockSpec((None, None, 1, 3 * D_MODEL),
                         lambda i: (layer, i // tiles_per_seq, 0, 0)),
            pl.BlockSpec((None, 1, D_MODEL), lambda i: (layer, 0, 0)),
            pl.BlockSpec((None, D_MODEL, IN_WIDTH), lambda i: (layer, 0, 0),
                         pipeline_mode=pl.Buffered(1)),
        ],
        out_specs=out_specs,
        out_shape=out_shape,
        compiler_params=_params("arbitrary"),
        name="in_projection",
    )(x, mod, norm_pre.reshape(DEPTH, 1, D_MODEL), w_in_bf)


def _outproj_kernel(*refs, n_parts):
    o_refs = refs[:n_parts]
    w_refs = refs[n_parts:2 * n_parts]
    x_ref, mod_ref, g_ref, out_ref = refs[2 * n_parts:]
    acc = _dot(o_refs[0][...], w_refs[0][...])
    for o_ref, w_ref in zip(o_refs[1:], w_refs[1:]):
        acc = acc + _dot(o_ref[...], w_ref[...])
    ms = jnp.mean(acc * acc, axis=-1, keepdims=True)
    y = (acc * lax.rsqrt(ms + EPS)) * g_ref[...]
    gate = mod_ref[:, 2 * D_MODEL:3 * D_MODEL]
    out_ref[...] = x_ref[...] + gate * y


def _outproj(parts, x, mod, norm_post, w_out_bf, layer):
    m = x.shape[0]
    n_seq = mod.shape[1]
    tiles_per_seq = m // n_seq // TOKEN_TILE
    in_specs, row = [], 0
    w_specs = []
    for p in parts:
        width = p.shape[1]
        assert row % width == 0
        in_specs.append(pl.BlockSpec((TOKEN_TILE, width), lambda i: (i, 0)))
        w_specs.append(pl.BlockSpec((None, width, D_MODEL),
                                    functools.partial(lambda i, b: (layer, b, 0), b=row // width)))
        row += width
    assert row == D_MODEL
    in_specs += w_specs + [
        pl.BlockSpec((TOKEN_TILE, D_MODEL), lambda i: (i, 0)),
        pl.BlockSpec((None, None, 1, 3 * D_MODEL),
                     lambda i: (layer, i // tiles_per_seq, 0, 0)),
        pl.BlockSpec((None, 1, D_MODEL), lambda i: (layer, 0, 0)),
    ]
    return pl.pallas_call(
        functools.partial(_outproj_kernel, n_parts=len(parts)),
        grid=(m // TOKEN_TILE,),
        in_specs=in_specs,
        out_specs=pl.BlockSpec((TOKEN_TILE, D_MODEL), lambda i: (i, 0)),
        out_shape=jax.ShapeDtypeStruct((m, D_MODEL), F32),
        compiler_params=_params("arbitrary"),
        name="out_projection",
    )(*parts, *([w_out_bf] * len(parts)), x, mod, norm_post.reshape(DEPTH, 1, D_MODEL))


def _softmax_pv(scores, sink, values):
    m = sink
    for s in scores:
        m = jnp.maximum(m, jnp.max(s, axis=-1, keepdims=True))
    denom = jnp.exp(sink - m)
    out = None
    for s, v in zip(scores, values):
        e = jnp.exp(s - m)
        denom = denom + jnp.sum(e, axis=-1, keepdims=True)
        pv = _dot(e.astype(BF16), v)
        out = pv if out is None else out + pv
    return out / denom


def _fourier_cols(u_ref, col0, dftc):
    tops, bots = [], []
    for g in range(FOURIER_GROUPS):
        c = col0 + g * FOURIER_GROUP_DIM
        y = _dot(u_ref[:, c:c + FOURIER_GROUP_DIM], dftc)
        tops.append(y[:, :FOURIER_GROUP_DIM].astype(BF16))
        bots.append(y[:, FOURIER_GROUP_DIM:].astype(BF16))
    return jnp.concatenate(tops, axis=1), jnp.concatenate(bots, axis=1)


def _ctx_even_kernel(sink_ref, p_ref, dftc_ref, dftt_ref, o_ref):
    scale = HEAD_DIM ** -0.5
    for h in range(N_KV_HEADS):
        k = p_ref[:, K_OFF + h * HEAD_DIM:K_OFF + (h + 1) * HEAD_DIM]
        v = p_ref[:, V_OFF + h * HEAD_DIM:V_OFF + (h + 1) * HEAD_DIM]
        for g in range(Q_PER_KV):
            hq = h * Q_PER_KV + g
            cols = slice(hq * HEAD_DIM, (hq + 1) * HEAD_DIM)
            s = _dot_nt(p_ref[:, cols], k) * scale
            o = _softmax_pv([s], sink_ref[hq], [v])
            ga = p_ref[:, GA_OFF + hq * HEAD_DIM:GA_OFF + (hq + 1) * HEAD_DIM].astype(F32)
            o_ref[:, cols] = (_silu(ga) * o).astype(BF16)
    top, bot = _fourier_cols(p_ref, UB_OFF, dftc_ref[...])
    f = _dot(dftt_ref[...], jnp.concatenate([top, bot], axis=0))
    gb = p_ref[:, GB_OFF:GB_OFF + FOURIER_WIDTH].astype(F32)
    o_ref[:, ATTN_WIDTH:] = (_silu(gb) * f).astype(BF16)


def _ctx_even(proj, sink, dftc, dftt, seq):
    m = proj.shape[0]
    return pl.pallas_call(
        _ctx_even_kernel,
        grid=(m // seq,),
        in_specs=[
            pl.BlockSpec(memory_space=pltpu.SMEM),
            pl.BlockSpec((seq, IN_WIDTH), lambda b: (b, 0)),
            pl.BlockSpec(dftc.shape, lambda b: (0, 0)),
            pl.BlockSpec(dftt.shape, lambda b: (0, 0)),
        ],
        out_specs=pl.BlockSpec((seq, D_MODEL), lambda b: (b, 0)),
        out_shape=jax.ShapeDtypeStruct((m, D_MODEL), BF16),
        compiler_params=_params("arbitrary"),
        name="context_attention_fourier",
    )(sink, proj, dftc, dftt)


def _rope(x, cos, sin_signed):
    lane = lax.broadcasted_iota(jnp.int32, x.shape, 1)
    partner = jnp.where((lane % 64) < 32,
                        pltpu.roll(x, HEAD_DIM - 32, 1), pltpu.roll(x, 32, 1))
    return x * cos + partner * sin_signed


def _lat_attn_kernel(sink_ref, row_ref, k_ref, v_ref, kc_ref, vc_ref, cos_ref, sin_ref,
                     o_ref, kr_ref, *, seq):
    i = pl.program_id(1)
    rope_rows = 256

    @pl.when(i == 0)
    def _():
        def body(c, carry):
            r = pl.multiple_of(c * rope_rows, rope_rows)
            cos = cos_ref[pl.ds(r, rope_rows), :]
            sin = sin_ref[pl.ds(r, rope_rows), :]
            for h in range(N_KV_HEADS):
                cols = slice(h * HEAD_DIM, (h + 1) * HEAD_DIM)
                kr_ref[pl.ds(r, rope_rows), cols] = _rope(
                    k_ref[pl.ds(r, rope_rows), cols].astype(F32), cos, sin).astype(BF16)
            return carry
        lax.fori_loop(0, seq // rope_rows, body, 0)

    scale = HEAD_DIM ** -0.5
    span = 3 * BLOCK
    q0 = pl.multiple_of(i * BLOCK, BLOCK)
    start = pl.multiple_of(jnp.clip(i * BLOCK - BLOCK, 0, seq - span), BLOCK)
    cos = cos_ref[pl.ds(q0, BLOCK), :]
    sin = sin_ref[pl.ds(q0, BLOCK), :]
    dist = ((q0 - start) + lax.broadcasted_iota(jnp.int32, (BLOCK, span), 0)
            - lax.broadcasted_iota(jnp.int32, (BLOCK, span), 1))
    valid = (dist >= -WINDOW) & (dist <= WINDOW)
    valid = jnp.concatenate([valid] * Q_PER_KV, axis=0)
    for h in range(N_KV_HEADS):
        kv_cols = slice(h * HEAD_DIM, (h + 1) * HEAD_DIM)
        q = jnp.concatenate(
            [_rope(row_ref[:, (h * Q_PER_KV + g) * HEAD_DIM:(h * Q_PER_KV + g + 1) * HEAD_DIM]
                   .astype(F32), cos, sin).astype(BF16) for g in range(Q_PER_KV)], axis=0)
        s_loc = _dot_nt(q, kr_ref[pl.ds(start, span), kv_cols]) * scale
        s_loc = jnp.where(valid, s_loc, NEG_INF)
        s_ctx = _dot_nt(q, kc_ref[:, kv_cols]) * scale
        sink = jnp.concatenate(
            [jnp.full((BLOCK, 1), sink_ref[h * Q_PER_KV + g], F32) for g in range(Q_PER_KV)],
            axis=0)
        o = _softmax_pv([s_loc, s_ctx], sink,
                        [v_ref[pl.ds(start, span), kv_cols], vc_ref[:, kv_cols]])
        for g in range(Q_PER_KV):
            hq = h * Q_PER_KV + g
            ga = row_ref[:, GA_OFF + hq * HEAD_DIM:GA_OFF + (hq + 1) * HEAD_DIM].astype(F32)
            o_ref[:, hq * HEAD_DIM:(hq + 1) * HEAD_DIM] = (
                _silu(ga) * o[g * BLOCK:(g + 1) * BLOCK]).astype(BF16)


def _lat_attn(proj, sink, kc, vc, cos, sin, seq):
    m = proj.shape[0]
    nb = seq // BLOCK
    return pl.pallas_call(
        functools.partial(_lat_attn_kernel, seq=seq),
        grid=(m // seq, nb),
        in_specs=[
            pl.BlockSpec(memory_space=pltpu.SMEM),
            pl.BlockSpec((BLOCK, IN_WIDTH), lambda b, i: (b * nb + i, 0)),
            pl.BlockSpec((seq, KV_WIDTH), lambda b, i: (b, K_OFF // KV_WIDTH)),
            pl.BlockSpec((seq, KV_WIDTH), lambda b, i: (b, V_OFF // KV_WIDTH)),
            pl.BlockSpec((None,) + kc.shape[1:], lambda b, i: (b, 0, 0)),
            pl.BlockSpec((None,) + vc.shape[1:], lambda b, i: (b, 0, 0)),
            pl.BlockSpec(cos.shape, lambda b, i: (0, 0)),
            pl.BlockSpec(sin.shape, lambda b, i: (0, 0)),
        ],
        out_specs=pl.BlockSpec((BLOCK, ATTN_WIDTH), lambda b, i: (b * nb + i, 0)),
        out_shape=jax.ShapeDtypeStruct((m, ATTN_WIDTH), BF16),
        scratch_shapes=[pltpu.VMEM((seq, KV_WIDTH), BF16)],
        compiler_params=_params("arbitrary", "arbitrary"),
        name="latent_window_attention",
    )(sink, proj, proj, proj, kc, vc, cos, sin)


def _lat_fourier_kernel(u_ref, gb_ref, dftc_ref, dftt_ref, o_ref, z_ref, *, seq):
    @pl.when(pl.program_id(1) == 0)
    def _():
        top, bot = _fourier_cols(u_ref, 0, dftc_ref[...])
        z_ref[0:seq, :] = top
        z_ref[seq:2 * seq, :] = bot

    f = _dot(dftt_ref[...], z_ref[...])
    o_ref[...] = (_silu(gb_ref[...].astype(F32)) * f).astype(BF16)


def _lat_fourier(proj, dftc, dftt, seq):
    m = proj.shape[0]
    nt = seq // FOURIER_ROW_TILE
    return pl.pallas_call(
        functools.partial(_lat_fourier_kernel, seq=seq),
        grid=(m // seq, nt),
        in_specs=[
            pl.BlockSpec((seq, FOURIER_WIDTH), lambda b, r: (b, UB_OFF // FOURIER_WIDTH)),
            pl.BlockSpec((FOURIER_ROW_TILE, FOURIER_WIDTH),
                         lambda b, r: (b * nt + r, GB_OFF // FOURIER_WIDTH)),
            pl.BlockSpec(dftc.shape, lambda b, r: (0, 0)),
            pl.BlockSpec((FOURIER_ROW_TILE, 2 * seq), lambda b, r: (r, 0)),
        ],
        out_specs=pl.BlockSpec((FOURIER_ROW_TILE, FOURIER_WIDTH), lambda b, r: (b * nt + r, 0)),
        out_shape=jax.ShapeDtypeStruct((m, FOURIER_WIDTH), BF16),
        scratch_shapes=[pltpu.VMEM((2 * seq, FOURIER_WIDTH), BF16)],
        compiler_params=_params("arbitrary", "arbitrary"),
        name="latent_fourier",
    )(proj, proj, dftc, dftt)


def _odd_kernel(uc_ref, gc_ref, ud_ref, vd_ref, gd_ref, pw_ref, ps_ref, sn_ref, sw_ref, sb_ref,
                o_ref, *, seq):
    r = pl.program_id(1)
    span = min(2 * SEQ_TILE, seq)
    t0 = pl.multiple_of(r * SEQ_TILE, SEQ_TILE)
    start = pl.multiple_of(jnp.clip(r * SEQ_TILE - SEQ_TILE // 2, 0, seq - span), SEQ_TILE // 2)
    rel = (lax.broadcasted_iota(jnp.int32, (SEQ_TILE, span), 1) + (start - t0)
           - lax.broadcasted_iota(jnp.int32, (SEQ_TILE, span), 0))
    pos = t0 + lax.broadcasted_iota(jnp.int32, (SEQ_TILE, 1), 0)
    for g, w in enumerate(POOL_WINDOWS):
        cols = slice(g * POOL_GROUP_DIM, (g + 1) * POOL_GROUP_DIM)
        band = jnp.where((rel >= -(w // 2)) & (rel < w // 2), 1.0, 0.0).astype(BF16)
        total = _dot(band, uc_ref[pl.ds(start, span), cols])
        cnt = (jnp.minimum(pos + w // 2, seq) - jnp.maximum(pos - w // 2, 0)).astype(F32)
        pooled = total / cnt - uc_ref[pl.ds(t0, SEQ_TILE), cols].astype(F32)
        mixed = _dot(pooled.astype(BF16), pw_ref[g]) * ps_ref[:, cols]
        o_ref[:, cols] = (_silu(gc_ref[:, cols].astype(F32)) * mixed).astype(BF16)
    for h in range(SGU_HEADS):
        cols = slice(h * SGU_GROUP_DIM, (h + 1) * SGU_GROUP_DIM)
        v = vd_ref[:, cols].astype(F32)
        mu = jnp.mean(v, axis=-1, keepdims=True)
        vc = v - mu
        var = jnp.mean(vc * vc, axis=-1, keepdims=True)
        vn = (vc * lax.rsqrt(var + EPS) * sn_ref[:, cols]).astype(BF16)
        bias = sb_ref[:, h:h + 1]
        for c in range(SEQ_TILE // SGU_CHUNK):
            rows = slice(c * SGU_CHUNK, (c + 1) * SGU_CHUNK)
            mixed = _dot(sw_ref[h], vn[rows]) + bias
            gated = ud_ref[rows, cols].astype(F32) * mixed
            o_ref[rows, POOL_WIDTH + h * SGU_GROUP_DIM:POOL_WIDTH + (h + 1) * SGU_GROUP_DIM] = (
                _silu(gd_ref[rows, cols].astype(F32)) * gated).astype(BF16)


def _odd_mixer(proj, pool_w_bf, pool_scale, sgu_norm, sgu_w_bf, sgu_b_t, j, seq):
    m = proj.shape[0]
    nt = seq // SEQ_TILE

    def tile_spec(off):
        return pl.BlockSpec((SEQ_TILE, POOL_WIDTH), lambda s, r: (s * nt + r, off // POOL_WIDTH))

    return pl.pallas_call(
        functools.partial(_odd_kernel, seq=seq),
        grid=(m // seq, nt),
        in_specs=[
            pl.BlockSpec((seq, POOL_WIDTH), lambda s, r: (s, UC_OFF // POOL_WIDTH)),
            tile_spec(GC_OFF), tile_spec(UD_OFF), tile_spec(VD_OFF), tile_spec(GD_OFF),
            pl.BlockSpec((None,) + pool_w_bf.shape[1:], lambda s, r: (j, 0, 0, 0)),
            pl.BlockSpec((None, 1, POOL_WIDTH), lambda s, r: (j, 0, 0)),
            pl.BlockSpec((None, 1, SGU_WIDTH), lambda s, r: (j, 0, 0)),
            pl.BlockSpec((None,) + sgu_w_bf.shape[1:], lambda s, r: (j, 0, 0, 0)),
            pl.BlockSpec((None,) + sgu_b_t.shape[1:], lambda s, r: (j, 0, 0)),
        ],
        out_specs=pl.BlockSpec((SEQ_TILE, D_MODEL), lambda s, r: (s * nt + r, 0)),
        out_shape=jax.ShapeDtypeStruct((m, D_MODEL), BF16),
        compiler_params=_params("arbitrary", "arbitrary"),
        name="pool_spatial_gating",
    )(proj, proj, proj, proj, proj, pool_w_bf, pool_scale.reshape(-1, 1, POOL_WIDTH),
      sgu_norm.reshape(-1, 1, SGU_WIDTH), sgu_w_bf, sgu_b_t)


def _dft_tables(seq):
    def cos_sin(n):
        idx = np.arange(n)
        ang = 2.0 * np.pi * ((idx[:, None] * idx[None, :]) % n) / n
        return np.cos(ang) / np.sqrt(n), np.sin(ang) / np.sqrt(n)

    cc, sc = cos_sin(FOURIER_GROUP_DIM)
    ct, st = cos_sin(seq)
    dftc = jnp.asarray(np.concatenate([cc, sc], axis=1), dtype=F32).astype(BF16)
    dftt = jnp.asarray(np.concatenate([ct, -st], axis=1), dtype=F32).astype(BF16)
    return dftc, dftt


def _rope_tables(seq):
    n_freq = HEAD_DIM // 4
    t = np.arange(seq)
    inv = ROPE_THETA ** (-np.arange(n_freq, dtype=np.float64) / n_freq)
    ang = np.stack([(t // GRID_W)[:, None] * inv, (t % GRID_W)[:, None] * inv], axis=1)
    cos = np.repeat(np.cos(ang)[:, :, None, :], 2, axis=2).reshape(seq, HEAD_DIM)
    sin = np.sin(ang)
    sin = np.stack([-sin, sin], axis=2).reshape(seq, HEAD_DIM)
    return jnp.asarray(cos, dtype=F32), jnp.asarray(sin, dtype=F32)


def kernel(x_prompt, x_sample, cache_k, cache_v, c, c_ctx, w_ada, b_ada, norm_pre, norm_post,
           w_in, w_out, attn_sink, pool_w, pool_scale, sgu_norm, sgu_w, sgu_b):
    batch, seq, _ = x_prompt.shape
    dec_batch, dec_seq, _ = x_sample.shape
    past = cache_k.shape[2]

    w_in_bf = w_in.astype(BF16)
    w_out_bf = w_out.astype(BF16)
    pool_w_bf = pool_w.astype(BF16)
    sgu_w_bf = sgu_w.astype(BF16)
    sgu_b_t = jnp.swapaxes(sgu_b, 1, 2)
    kc = cache_k.reshape(dec_batch, -1, past, KV_WIDTH).astype(BF16)
    vc = cache_v.reshape(dec_batch, -1, past, KV_WIDTH).astype(BF16)

    conds = jnp.concatenate(
        [c_ctx[None, :], c, jnp.zeros((COND_ROWS - 1 - dec_batch, D_MODEL), F32)], axis=0)
    mod = _ada(conds, w_ada, b_ada)
    mod_p = mod[:, 0:1].reshape(DEPTH, 1, 1, 3 * D_MODEL)
    mod_s = mod[:, 1:1 + dec_batch].reshape(DEPTH, dec_batch, 1, 3 * D_MODEL)

    dftc, dftt_p = _dft_tables(seq)
    _, dftt_s = _dft_tables(dec_seq)
    cos, sin = _rope_tables(dec_seq)

    xp = x_prompt.reshape(batch * seq, D_MODEL)
    xs = x_sample.reshape(dec_batch * dec_seq, D_MODEL)
    new_k, new_v = [], []
    for l in range(DEPTH):
        j = l // 2
        if l % 2 == 0:
            proj_p, k_p, v_p = _inproj(xp, mod_p, norm_pre, w_in_bf, l, True)
            proj_s, = _inproj(xs, mod_s, norm_pre, w_in_bf, l, False)
            new_k.append(k_p)
            new_v.append(v_p)
            parts_p = [_ctx_even(proj_p, attn_sink[j], dftc, dftt_p, seq)]
            parts_s = [_lat_attn(proj_s, attn_sink[j], kc[:, j], vc[:, j], cos, sin, dec_seq),
                       _lat_fourier(proj_s, dftc, dftt_s, dec_seq)]
        else:
            proj_p, = _inproj(xp, mod_p, norm_pre, w_in_bf, l, False)
            proj_s, = _inproj(xs, mod_s, norm_pre, w_in_bf, l, False)
            parts_p = [_odd_mixer(proj_p, pool_w_bf, pool_scale, sgu_norm, sgu_w_bf, sgu_b_t, j, seq)]
            parts_s = [_odd_mixer(proj_s, pool_w_bf, pool_scale, sgu_norm, sgu_w_bf, sgu_b_t, j,
                                  dec_seq)]
        xp = _outproj(parts_p, xp, mod_p, norm_post, w_out_bf, l)
        xs = _outproj(parts_s, xs, mod_s, norm_post, w_out_bf, l)

    def stack_cache(parts):
        return jnp.stack(parts, axis=1).reshape(batch, len(parts), seq, N_KV_HEADS, HEAD_DIM)

    new_k = [p.reshape(batch, seq, KV_WIDTH) for p in new_k]
    new_v = [p.reshape(batch, seq, KV_WIDTH) for p in new_v]
    return (xp.reshape(batch, seq, D_MODEL), xs.reshape(dec_batch, dec_seq, D_MODEL),
            stack_cache(new_k), stack_cache(new_v))
```

```python
import functools

import numpy as np
import jax
import jax.numpy as jnp
from jax import lax
from jax.experimental import pallas as pl
from jax.experimental.pallas import tpu as pltpu

F32 = jnp.float32
BF16 = jnp.bfloat16

D_MODEL = 2048
DEPTH = 4
GRID_W = 64
HEAD_DIM = 128
N_Q_HEADS = 12
N_KV_HEADS = 4
Q_PER_KV = N_Q_HEADS // N_KV_HEADS
ATTN_WIDTH = N_Q_HEADS * HEAD_DIM
KV_WIDTH = N_KV_HEADS * HEAD_DIM
WINDOW = 128
BLOCK = 128
ROPE_THETA = 10000.0
FOURIER_GROUPS = 4
FOURIER_WIDTH = D_MODEL // 4
FOURIER_GROUP_DIM = FOURIER_WIDTH // FOURIER_GROUPS
POOL_WINDOWS = (2, 4, 8, 16)
POOL_WIDTH = D_MODEL // 2
POOL_GROUP_DIM = POOL_WIDTH // len(POOL_WINDOWS)
SGU_HEADS = 4
SGU_WIDTH = D_MODEL // 2
SGU_GROUP_DIM = SGU_WIDTH // SGU_HEADS
SGU_CHUNK = 128
IN_WIDTH = 2 * ATTN_WIDTH + 2 * KV_WIDTH + 2 * FOURIER_WIDTH
EPS = 1e-6
NEG_INF = -1e30
ATTN_SCALE = HEAD_DIM ** -0.5

Q_OFF, K_OFF, V_OFF = 0, ATTN_WIDTH, ATTN_WIDTH + KV_WIDTH
GA_OFF = ATTN_WIDTH + 2 * KV_WIDTH
UB_OFF = GA_OFF + ATTN_WIDTH
GB_OFF = UB_OFF + FOURIER_WIDTH
UC_OFF, GC_OFF, UD_OFF, VD_OFF, GD_OFF = (i * POOL_WIDTH for i in range(5))

V7X_VMEM_BYTES = 64 * 1024 * 1024
VMEM_LIMIT = V7X_VMEM_BYTES - 8 * 1024 * 1024

TOKEN_TILE = 512
PROJ_CHUNK = 512
SEQ_TILE = 256
ADA_TILE = 1024
COND_ROWS = 8


def _silu(x):
    return x / (1.0 + jnp.exp(-x))


def _params(*sem):
    return pltpu.CompilerParams(dimension_semantics=sem, vmem_limit_bytes=VMEM_LIMIT)


def _dot(a, b):
    return jnp.dot(a, b, preferred_element_type=F32)


def _dot_nt(a, b):
    return lax.dot_general(a, b, (((1,), (1,)), ((), ())), preferred_element_type=F32)


def _ada_kernel(c_ref, w_ref, b_ref, o_ref):
    s = _silu(c_ref[...]).astype(BF16)
    o_ref[...] = _dot(s, w_ref[...].astype(BF16)) + b_ref[...]


def _ada(conds, w_ada, b_ada):
    n = w_ada.shape[-1]
    return pl.pallas_call(
        _ada_kernel,
        grid=(DEPTH, n // ADA_TILE),
        in_specs=[
            pl.BlockSpec((COND_ROWS, D_MODEL), lambda l, j: (0, 0)),
            pl.BlockSpec((None, D_MODEL, ADA_TILE), lambda l, j: (l, 0, j)),
            pl.BlockSpec((None, 1, ADA_TILE), lambda l, j: (l, 0, j)),
        ],
        out_specs=pl.BlockSpec((None, COND_ROWS, ADA_TILE), lambda l, j: (l, 0, j)),
        out_shape=jax.ShapeDtypeStruct((DEPTH, COND_ROWS, n), F32),
        compiler_params=_params("arbitrary", "arbitrary"),
        name="ada_modulation",
    )(conds, w_ada, b_ada.reshape(DEPTH, 1, n))


def _inproj_kernel(x_ref, mod_ref, g_ref, w_ref, *rest, q_scale, n_cache, cache_slot):
    proj_ref = rest[n_cache]
    kv_refs = rest[n_cache + 1:]
    x = x_ref[...]
    shift = mod_ref[:, 0:D_MODEL]
    scale = mod_ref[:, D_MODEL:2 * D_MODEL]
    ms = jnp.mean(x * x, axis=-1, keepdims=True)
    h = (x * lax.rsqrt(ms + EPS)) * g_ref[...]
    h = (h * (1.0 + scale) + shift).astype(BF16)
    for c0 in range(0, IN_WIDTH, PROJ_CHUNK):
        acc = _dot(h, w_ref[:, c0:c0 + PROJ_CHUNK])
        if kv_refs and c0 in (K_OFF, V_OFF):
            kv_ref = kv_refs[(c0 - K_OFF) // KV_WIDTH]
            seq = kv_ref.shape[-2]
            for s in range(kv_ref.shape[0]):
                rows = acc[s * seq:(s + 1) * seq]
                if kv_ref.ndim == 3:
                    kv_ref[s] = rows
                else:
                    for slot in range(kv_ref.shape[1]):
                        kv_ref[s, slot] = rows if slot == cache_slot else jnp.zeros_like(rows)
        if q_scale is not None and c0 < ATTN_WIDTH:
            acc = acc * q_scale
        proj_ref[:, c0:c0 + PROJ_CHUNK] = acc.astype(BF16)


def _inproj(x, mod, norm_pre, w_in_bf, layer, q_scale=None, cache=None, cache_shape=None):
    assert KV_WIDTH == PROJ_CHUNK and K_OFF % PROJ_CHUNK == 0 and ATTN_WIDTH % PROJ_CHUNK == 0
    m = x.shape[0]
    n_seq = mod.shape[1]
    tiles_per_seq = m // n_seq // TOKEN_TILE
    in_specs = [
        pl.BlockSpec((TOKEN_TILE, D_MODEL), lambda i: (i, 0)),
        pl.BlockSpec((None, None, 1, 3 * D_MODEL), lambda i: (layer, i // tiles_per_seq, 0, 0)),
        pl.BlockSpec((None, 1, D_MODEL), lambda i: (layer, 0, 0)),
        pl.BlockSpec((None, D_MODEL, IN_WIDTH), lambda i: (layer, 0, 0),
                     pipeline_mode=pl.Buffered(1)),
    ]
    args = [x, mod, norm_pre.reshape(DEPTH, 1, D_MODEL), w_in_bf]
    out_shape = [jax.ShapeDtypeStruct((m, IN_WIDTH), BF16)]
    out_specs = [pl.BlockSpec((TOKEN_TILE, IN_WIDTH), lambda i: (i, 0))]
    aliases = {}
    n_cache = 0
    if cache_shape is not None:
        n_slots, seq = cache_shape[1:3]
        slot = layer // 2
        if cache is None:
            kv_block = pl.BlockSpec((TOKEN_TILE // seq, n_slots, seq, KV_WIDTH),
                                    lambda i: (i, 0, 0, 0))
        else:
            kv_block = pl.BlockSpec((TOKEN_TILE // seq, None, seq, KV_WIDTH),
                                    lambda i: (i, slot, 0, 0))
        out_shape += [jax.ShapeDtypeStruct(cache_shape, F32)] * 2
        out_specs += [kv_block] * 2
        if cache is not None:
            n_cache = len(cache)
            aliases = {len(args) + t: 1 + t for t in range(n_cache)}
            in_specs += [pl.BlockSpec(memory_space=pl.ANY)] * n_cache
            args += list(cache)
    return pl.pallas_call(
        functools.partial(_inproj_kernel, q_scale=q_scale, n_cache=n_cache,
                          cache_slot=layer // 2),
        grid=(m // TOKEN_TILE,),
        in_specs=in_specs,
        out_specs=out_specs,
        out_shape=out_shape,
        input_output_aliases=aliases,
        compiler_params=_params("arbitrary"),
        name="in_projection",
    )(*args)


def _mixer_call(mixer, n_tiles, in_specs, args, scratch, x, mod, norm_post, w_out_bf, layer,
                rows, mod_row, name):
    n_in = len(in_specs)

    def mixer_tile(t):
        return jnp.minimum(t, n_tiles - 1)

    def tail_tile(t):
        return jnp.maximum(t - 1, 0)

    def clamp(spec):
        if spec.index_map is None:
            return spec
        return pl.BlockSpec(spec.block_shape, lambda t: spec.index_map(mixer_tile(t)),
                            pipeline_mode=spec.pipeline_mode)

    tail_specs = [
        pl.BlockSpec((None, D_MODEL, D_MODEL), lambda t: (layer, 0, 0),
                     pipeline_mode=pl.Buffered(1)),
        pl.BlockSpec((rows, D_MODEL), lambda t: (tail_tile(t), 0)),
        pl.BlockSpec((None, None, 1, 3 * D_MODEL),
                     lambda t: (layer, mod_row(tail_tile(t)), 0, 0)),
        pl.BlockSpec((None, 1, D_MODEL), lambda t: (layer, 0, 0)),
    ]

    def body(*refs):
        mixer_refs = refs[:n_in]
        w_ref, x_ref, mod_ref, g_ref, out_ref, o_even, o_odd = refs[n_in:n_in + 7]
        extra = refs[n_in + 7:]
        t = pl.program_id(0)

        @pl.when(t == 0)
        def _():
            o_odd[...] = jnp.zeros_like(o_odd)

        def step(o_read, o_write):
            acc = _dot(o_read[...], w_ref[...])
            ms = jnp.mean(acc * acc, axis=-1, keepdims=True)
            y = (acc * lax.rsqrt(ms + EPS)) * g_ref[...]
            out_ref[...] = x_ref[...] + mod_ref[:, 2 * D_MODEL:3 * D_MODEL] * y
            mixer(*mixer_refs, o_write, *extra, tile=mixer_tile(t))

        pl.when(t % 2 == 0)(lambda: step(o_odd, o_even))
        pl.when(t % 2 == 1)(lambda: step(o_even, o_odd))

    return pl.pallas_call(
        body,
        grid=(n_tiles + 1,),
        in_specs=[clamp(s) for s in in_specs] + tail_specs,
        out_specs=pl.BlockSpec((rows, D_MODEL), lambda t: (tail_tile(t), 0)),
        out_shape=jax.ShapeDtypeStruct(x.shape, F32),
        scratch_shapes=[pltpu.VMEM((rows, D_MODEL), BF16)] * 2 + list(scratch),
        compiler_params=_params("arbitrary"),
        name=name,
    )(*args, w_out_bf, x, mod, norm_post.reshape(DEPTH, 1, D_MODEL))


def _softmax_pv(scores, sink, values):
    m = sink
    for s in scores:
        m = jnp.maximum(m, jnp.max(s, axis=-1, keepdims=True))
    denom = jnp.exp(sink - m)
    out = None
    for s, v in zip(scores, values):
        e = jnp.exp(s - m)
        denom = denom + jnp.sum(e, axis=-1, keepdims=True)
        pv = _dot(e.astype(BF16), v)
        out = pv if out is None else out + pv
    return out / denom


def _fourier_cols(u_ref, col0, dftc):
    tops, bots = [], []
    for g in range(FOURIER_GROUPS):
        c = col0 + g * FOURIER_GROUP_DIM
        y = _dot(u_ref[:, c:c + FOURIER_GROUP_DIM], dftc)
        tops.append(y[:, :FOURIER_GROUP_DIM].astype(BF16))
        bots.append(y[:, FOURIER_GROUP_DIM:].astype(BF16))
    return jnp.concatenate(tops, axis=1), jnp.concatenate(bots, axis=1)


def _rope(x, cos, sin_signed):
    lane = lax.broadcasted_iota(jnp.int32, x.shape, 1)
    partner = jnp.where((lane % 64) < 32,
                        pltpu.roll(x, HEAD_DIM - 32, 1), pltpu.roll(x, 32, 1))
    return x * cos + partner * sin_signed


def _ctx_even_mixer(sink_ref, p_ref, dftc_ref, dftt_ref, o_ref, *, tile):
    del tile
    for h in range(N_KV_HEADS):
        k = p_ref[:, K_OFF + h * HEAD_DIM:K_OFF + (h + 1) * HEAD_DIM]
        v = p_ref[:, V_OFF + h * HEAD_DIM:V_OFF + (h + 1) * HEAD_DIM]
        for g in range(Q_PER_KV):
            hq = h * Q_PER_KV + g
            cols = slice(hq * HEAD_DIM, (hq + 1) * HEAD_DIM)
            s = _dot_nt(p_ref[:, cols], k)
            o = _softmax_pv([s], sink_ref[hq], [v])
            ga = p_ref[:, GA_OFF + hq * HEAD_DIM:GA_OFF + (hq + 1) * HEAD_DIM].astype(F32)
            o_ref[:, cols] = (_silu(ga) * o).astype(BF16)
    top, bot = _fourier_cols(p_ref, UB_OFF, dftc_ref[...])
    f = _dot(dftt_ref[...], jnp.concatenate([top, bot], axis=0))
    gb = p_ref[:, GB_OFF:GB_OFF + FOURIER_WIDTH].astype(F32)
    o_ref[:, ATTN_WIDTH:] = (_silu(gb) * f).astype(BF16)


def _ctx_even(proj, sink, dftc, dftt, seq, **tail):
    m = proj.shape[0]
    return _mixer_call(
        _ctx_even_mixer, m // seq,
        [pl.BlockSpec(memory_space=pltpu.SMEM),
         pl.BlockSpec((seq, IN_WIDTH), lambda b: (b, 0)),
         pl.BlockSpec(dftc.shape, lambda b: (0, 0)),
         pl.BlockSpec(dftt.shape, lambda b: (0, 0))],
        [sink, proj, dftc, dftt], [],
        rows=seq, mod_row=lambda b: 0, name="context_attention_fourier", **tail)


def _lat_even_mixer(sink_ref, row_ref, k_ref, v_ref, kc_ref, vc_ref, cos_ref, sin_ref,
                    u_ref, dftc_ref, dftt_ref, o_ref, kr_ref, z_ref, *, tile, seq):
    i = tile % (seq // BLOCK)
    rope_rows = 256

    @pl.when(i == 0)
    def _():
        def body(c, carry):
            r = pl.multiple_of(c * rope_rows, rope_rows)
            cos = cos_ref[pl.ds(r, rope_rows), :]
            sin = sin_ref[pl.ds(r, rope_rows), :]
            for h in range(N_KV_HEADS):
                cols = slice(h * HEAD_DIM, (h + 1) * HEAD_DIM)
                kr_ref[pl.ds(r, rope_rows), cols] = _rope(
                    k_ref[pl.ds(r, rope_rows), cols].astype(F32), cos, sin).astype(BF16)
            return carry
        lax.fori_loop(0, seq // rope_rows, body, 0)
        top, bot = _fourier_cols(u_ref, 0, dftc_ref[...])
        z_ref[0:seq, :] = top
        z_ref[seq:2 * seq, :] = bot

    span = 3 * BLOCK
    q0 = pl.multiple_of(i * BLOCK, BLOCK)
    start = pl.multiple_of(jnp.clip(i * BLOCK - BLOCK, 0, seq - span), BLOCK)
    cos = cos_ref[pl.ds(q0, BLOCK), :]
    sin = sin_ref[pl.ds(q0, BLOCK), :]
    dist = ((q0 - start) + lax.broadcasted_iota(jnp.int32, (BLOCK, span), 0)
            - lax.broadcasted_iota(jnp.int32, (BLOCK, span), 1))
    valid = (dist >= -WINDOW) & (dist <= WINDOW)
    valid = jnp.concatenate([valid] * Q_PER_KV, axis=0)
    for h in range(N_KV_HEADS):
        kv_cols = slice(h * HEAD_DIM, (h + 1) * HEAD_DIM)
        q = jnp.concatenate(
            [_rope(row_ref[:, (h * Q_PER_KV + g) * HEAD_DIM:(h * Q_PER_KV + g + 1) * HEAD_DIM]
                   .astype(F32), cos, sin).astype(BF16) for g in range(Q_PER_KV)], axis=0)
        s_loc = _dot_nt(q, kr_ref[pl.ds(start, span), kv_cols])
        s_loc = jnp.where(valid, s_loc, NEG_INF)
        s_ctx = _dot_nt(q, kc_ref[:, kv_cols])
        sink = jnp.concatenate(
            [jnp.full((BLOCK, 1), sink_ref[h * Q_PER_KV + g], F32) for g in range(Q_PER_KV)],
            axis=0)
        o = _softmax_pv([s_loc, s_ctx], sink,
                        [v_ref[pl.ds(start, span), kv_cols], vc_ref[:, kv_cols]])
        for g in range(Q_PER_KV):
            hq = h * Q_PER_KV + g
            ga = row_ref[:, GA_OFF + hq * HEAD_DIM:GA_OFF + (hq + 1) * HEAD_DIM].astype(F32)
            o_ref[:, hq * HEAD_DIM:(hq + 1) * HEAD_DIM] = (
                _silu(ga) * o[g * BLOCK:(g + 1) * BLOCK]).astype(BF16)
    f = _dot(dftt_ref[...], z_ref[...])
    gb = row_ref[:, GB_OFF:GB_OFF + FOURIER_WIDTH].astype(F32)
    o_ref[:, ATTN_WIDTH:] = (_silu(gb) * f).astype(BF16)


def _lat_even(proj, sink, kc, vc, cos, sin, dftc, dftt, seq, **tail):
    m = proj.shape[0]
    nb = seq // BLOCK
    return _mixer_call(
        functools.partial(_lat_even_mixer, seq=seq), m // BLOCK,
        [pl.BlockSpec(memory_space=pltpu.SMEM),
         pl.BlockSpec((BLOCK, IN_WIDTH), lambda t: (t, 0)),
         pl.BlockSpec((seq, KV_WIDTH), lambda t: (t // nb, K_OFF // KV_WIDTH)),
         pl.BlockSpec((seq, KV_WIDTH), lambda t: (t // nb, V_OFF // KV_WIDTH)),
         pl.BlockSpec((None,) + kc.shape[1:], lambda t: (t // nb, 0, 0)),
         pl.BlockSpec((None,) + vc.shape[1:], lambda t: (t // nb, 0, 0)),
         pl.BlockSpec(cos.shape, lambda t: (0, 0)),
         pl.BlockSpec(sin.shape, lambda t: (0, 0)),
         pl.BlockSpec((seq, FOURIER_WIDTH), lambda t: (t // nb, UB_OFF // FOURIER_WIDTH)),
         pl.BlockSpec(dftc.shape, lambda t: (0, 0)),
         pl.BlockSpec((BLOCK, 2 * seq), lambda t: (t % nb, 0))],
        [sink, proj, proj, proj, kc, vc, cos, sin, proj, dftc, dftt],
        [pltpu.VMEM((seq, KV_WIDTH), BF16), pltpu.VMEM((2 * seq, FOURIER_WIDTH), BF16)],
        rows=BLOCK, mod_row=lambda t: t // nb, name="latent_attention_fourier", **tail)


def _odd_mixer_body(uc_ref, gc_ref, ud_ref, vd_ref, gd_ref, pw_ref, ps_ref, sn_ref, sw_ref,
                    sb_ref, o_ref, *, tile, seq):
    r = tile % (seq // SEQ_TILE)
    span = min(2 * SEQ_TILE, seq)
    t0 = pl.multiple_of(r * SEQ_TILE, SEQ_TILE)
    start = pl.multiple_of(jnp.clip(r * SEQ_TILE - SEQ_TILE // 2, 0, seq - span), SEQ_TILE // 2)
    rel = (lax.broadcasted_iota(jnp.int32, (SEQ_TILE, span), 1) + (start - t0)
           - lax.broadcasted_iota(jnp.int32, (SEQ_TILE, span), 0))
    pos = t0 + lax.broadcasted_iota(jnp.int32, (SEQ_TILE, 1), 0)
    for g, w in enumerate(POOL_WINDOWS):
        cols = slice(g * POOL_GROUP_DIM, (g + 1) * POOL_GROUP_DIM)
        band = jnp.where((rel >= -(w // 2)) & (rel < w // 2), 1.0, 0.0).astype(BF16)
        total = _dot(band, uc_ref[pl.ds(start, span), cols])
        cnt = (jnp.minimum(pos + w // 2, seq) - jnp.maximum(pos - w // 2, 0)).astype(F32)
        pooled = total / cnt - uc_ref[pl.ds(t0, SEQ_TILE), cols].astype(F32)
        mixed = _dot(pooled.astype(BF16), pw_ref[g]) * ps_ref[:, cols]
        o_ref[:, cols] = (_silu(gc_ref[:, cols].astype(F32)) * mixed).astype(BF16)
    for h in range(SGU_HEADS):
        cols = slice(h * SGU_GROUP_DIM, (h + 1) * SGU_GROUP_DIM)
        v = vd_ref[:, cols].astype(F32)
        mu = jnp.mean(v, axis=-1, keepdims=True)
        vc = v - mu
        var = jnp.mean(vc * vc, axis=-1, keepdims=True)
        vn = (vc * lax.rsqrt(var + EPS) * sn_ref[:, cols]).astype(BF16)
        bias = sb_ref[:, h:h + 1]
        for c in range(SEQ_TILE // SGU_CHUNK):
            rows = slice(c * SGU_CHUNK, (c + 1) * SGU_CHUNK)
            gated = ud_ref[rows, cols].astype(F32) * (_dot(sw_ref[h], vn[rows]) + bias)
            o_ref[rows, POOL_WIDTH + h * SGU_GROUP_DIM:POOL_WIDTH + (h + 1) * SGU_GROUP_DIM] = (
                _silu(gd_ref[rows, cols].astype(F32)) * gated).astype(BF16)


def _odd_mixer(proj, pool_w_bf, pool_scale, sgu_norm, sgu_w_bf, sgu_b_t, j, seq, n_mod, **tail):
    m = proj.shape[0]
    nt = seq // SEQ_TILE

    def tile_spec(off):
        return pl.BlockSpec((SEQ_TILE, POOL_WIDTH), lambda t: (t, off // POOL_WIDTH))

    return _mixer_call(
        functools.partial(_odd_mixer_body, seq=seq), m // SEQ_TILE,
        [pl.BlockSpec((seq, POOL_WIDTH), lambda t: (t // nt, UC_OFF // POOL_WIDTH)),
         tile_spec(GC_OFF), tile_spec(UD_OFF), tile_spec(VD_OFF), tile_spec(GD_OFF),
         pl.BlockSpec((None,) + pool_w_bf.shape[1:], lambda t: (j, 0, 0, 0)),
         pl.BlockSpec((None, 1, POOL_WIDTH), lambda t: (j, 0, 0)),
         pl.BlockSpec((None, 1, SGU_WIDTH), lambda t: (j, 0, 0)),
         pl.BlockSpec((None,) + sgu_w_bf.shape[1:], lambda t: (j, 0, 0, 0)),
         pl.BlockSpec((None,) + sgu_b_t.shape[1:], lambda t: (j, 0, 0))],
        [proj, proj, proj, proj, proj, pool_w_bf, pool_scale.reshape(-1, 1, POOL_WIDTH),
         sgu_norm.reshape(-1, 1, SGU_WIDTH), sgu_w_bf, sgu_b_t], [],
        rows=SEQ_TILE, mod_row=(lambda t: t // nt) if n_mod > 1 else (lambda t: 0),
        name="pool_spatial_gating", **tail)


def _dft_tables(seq):
    def cos_sin(n):
        idx = np.arange(n)
        ang = 2.0 * np.pi * ((idx[:, None] * idx[None, :]) % n) / n
        return np.cos(ang) / np.sqrt(n), np.sin(ang) / np.sqrt(n)

    cc, sc = cos_sin(FOURIER_GROUP_DIM)
    ct, st = cos_sin(seq)
    dftc = jnp.asarray(np.concatenate([cc, sc], axis=1), dtype=F32).astype(BF16)
    dftt = jnp.asarray(np.concatenate([ct, -st], axis=1), dtype=F32).astype(BF16)
    return dftc, dftt


def _rope_tables(seq):
    n_freq = HEAD_DIM // 4
    t = np.arange(seq)
    inv = ROPE_THETA ** (-np.arange(n_freq, dtype=np.float64) / n_freq)
    ang = np.stack([(t // GRID_W)[:, None] * inv, (t % GRID_W)[:, None] * inv], axis=1)
    cos = np.repeat(np.cos(ang)[:, :, None, :], 2, axis=2).reshape(seq, HEAD_DIM)
    sin = np.sin(ang)
    sin = np.stack([-sin, sin], axis=2).reshape(seq, HEAD_DIM)
    return jnp.asarray(cos, dtype=F32), jnp.asarray(sin, dtype=F32)


def kernel(x_prompt, x_sample, cache_k, cache_v, c, c_ctx, w_ada, b_ada, norm_pre, norm_post,
           w_in, w_out, attn_sink, pool_w, pool_scale, sgu_norm, sgu_w, sgu_b):
    batch, seq, _ = x_prompt.shape
    dec_batch, dec_seq, _ = x_sample.shape
    n_attn, past = cache_k.shape[1], cache_k.shape[2]

    w_in_bf = w_in.astype(BF16)
    w_out_bf = w_out.astype(BF16)
    pool_w_bf = pool_w.astype(BF16)
    sgu_w_bf = sgu_w.astype(BF16)
    sgu_b_t = jnp.swapaxes(sgu_b, 1, 2)
    kc = cache_k.reshape(dec_batch, n_attn, past, KV_WIDTH).astype(BF16)
    vc = cache_v.reshape(dec_batch, n_attn, past, KV_WIDTH).astype(BF16)

    conds = jnp.concatenate(
        [c_ctx[None, :], c, jnp.zeros((COND_ROWS - 1 - dec_batch, D_MODEL), F32)], axis=0)
    mod = _ada(conds, w_ada, b_ada)
    mod_p = mod[:, 0:1].reshape(DEPTH, 1, 1, 3 * D_MODEL)
    mod_s = mod[:, 1:1 + dec_batch].reshape(DEPTH, dec_batch, 1, 3 * D_MODEL)

    dftc, dftt_p = _dft_tables(seq)
    _, dftt_s = _dft_tables(dec_seq)
    cos, sin = _rope_tables(dec_seq)

    xp = x_prompt.reshape(batch * seq, D_MODEL)
    xs = x_sample.reshape(dec_batch * dec_seq, D_MODEL)
    cache = None
    cache_shape = (batch, n_attn, seq, KV_WIDTH)
    for l in range(DEPTH):
        j = l // 2
        tail_p = dict(x=xp, mod=mod_p, norm_post=norm_post, w_out_bf=w_out_bf, layer=l)
        tail_s = dict(x=xs, mod=mod_s, norm_post=norm_post, w_out_bf=w_out_bf, layer=l)
        if l % 2 == 0:
            proj_p, *cache = _inproj(xp, mod_p, norm_pre, w_in_bf, l, q_scale=ATTN_SCALE,
                                     cache=cache, cache_shape=cache_shape)
            proj_s, = _inproj(xs, mod_s, norm_pre, w_in_bf, l, q_scale=ATTN_SCALE)
            xp = _ctx_even(proj_p, attn_sink[j], dftc, dftt_p, seq, **tail_p)
            xs = _lat_even(proj_s, attn_sink[j], kc[:, j], vc[:, j], cos, sin, dftc, dftt_s,
                           dec_seq, **tail_s)
        else:
            proj_p, = _inproj(xp, mod_p, norm_pre, w_in_bf, l)
            proj_s, = _inproj(xs, mod_s, norm_pre, w_in_bf, l)
            odd = (pool_w_bf, pool_scale, sgu_norm, sgu_w_bf, sgu_b_t, j)
            xp = _odd_mixer(proj_p, *odd, seq, 1, **tail_p)
            xs = _odd_mixer(proj_s, *odd, dec_seq, dec_batch, **tail_s)

    new_k, new_v = (t.reshape(batch, n_attn, seq, N_KV_HEADS, HEAD_DIM) for t in cache)
    return (xp.reshape(batch, seq, D_MODEL), xs.reshape(dec_batch, dec_seq, D_MODEL), new_k, new_v)
```

```python
import functools

import numpy as np
import jax
import jax.numpy as jnp
from jax import lax
from jax.experimental import pallas as pl
from jax.experimental.pallas import tpu as pltpu

F32 = jnp.float32
BF16 = jnp.bfloat16

D_MODEL = 2048
DEPTH = 4
GRID_W = 64
HEAD_DIM = 128
N_Q_HEADS = 12
N_KV_HEADS = 4
Q_PER_KV = N_Q_HEADS // N_KV_HEADS
ATTN_WIDTH = N_Q_HEADS * HEAD_DIM
KV_WIDTH = N_KV_HEADS * HEAD_DIM
WINDOW = 128
BLOCK = 128
ROPE_THETA = 10000.0
FOURIER_GROUPS = 4
FOURIER_WIDTH = D_MODEL // 4
FOURIER_GROUP_DIM = FOURIER_WIDTH // FOURIER_GROUPS
POOL_WINDOWS = (2, 4, 8, 16)
POOL_WIDTH = D_MODEL // 2
POOL_GROUP_DIM = POOL_WIDTH // len(POOL_WINDOWS)
SGU_HEADS = 4
SGU_WIDTH = D_MODEL // 2
SGU_GROUP_DIM = SGU_WIDTH // SGU_HEADS
SGU_CHUNK = 128
IN_WIDTH = 2 * ATTN_WIDTH + 2 * KV_WIDTH + 2 * FOURIER_WIDTH
EPS = 1e-6
NEG_INF = -1e30
ATTN_SCALE = HEAD_DIM ** -0.5

Q_OFF, K_OFF, V_OFF = 0, ATTN_WIDTH, ATTN_WIDTH + KV_WIDTH
GA_OFF = ATTN_WIDTH + 2 * KV_WIDTH
UB_OFF = GA_OFF + ATTN_WIDTH
GB_OFF = UB_OFF + FOURIER_WIDTH
UC_OFF, GC_OFF, UD_OFF, VD_OFF, GD_OFF = (i * POOL_WIDTH for i in range(5))

V7X_VMEM_BYTES = 64 * 1024 * 1024
VMEM_LIMIT = V7X_VMEM_BYTES - 8 * 1024 * 1024

TOKEN_TILE = 512
PROJ_CHUNK = 512
SEQ_TILE = 256
ADA_TILE = 1024
OUT_CHUNK = 256
COND_ROWS = 8


def _silu(x):
    return x / (1.0 + jnp.exp(-x))


def _params(*sem):
    return pltpu.CompilerParams(dimension_semantics=sem, vmem_limit_bytes=VMEM_LIMIT)


def _dot(a, b):
    return jnp.dot(a, b, preferred_element_type=F32)


def _dot_nt(a, b):
    return lax.dot_general(a, b, (((1,), (1,)), ((), ())), preferred_element_type=F32)


def _ada_kernel(c_ref, w_ref, b_ref, o_ref):
    s = _silu(c_ref[...]).astype(BF16)
    o_ref[...] = _dot(s, w_ref[...].astype(BF16)) + b_ref[...]


def _ada(conds, w_ada, b_ada):
    n = w_ada.shape[-1]
    return pl.pallas_call(
        _ada_kernel,
        grid=(DEPTH, n // ADA_TILE),
        in_specs=[
            pl.BlockSpec((COND_ROWS, D_MODEL), lambda l, j: (0, 0)),
            pl.BlockSpec((None, D_MODEL, ADA_TILE), lambda l, j: (l, 0, j)),
            pl.BlockSpec((None, 1, ADA_TILE), lambda l, j: (l, 0, j)),
        ],
        out_specs=pl.BlockSpec((None, COND_ROWS, ADA_TILE), lambda l, j: (l, 0, j)),
        out_shape=jax.ShapeDtypeStruct((DEPTH, COND_ROWS, n), F32),
        compiler_params=_params("arbitrary", "arbitrary"),
        name="ada_modulation",
    )(conds, w_ada, b_ada.reshape(DEPTH, 1, n))


def _inproj_kernel(x_ref, mod_ref, g_ref, w_ref, *rest, q_scale, n_cache, cache_slot):
    proj_ref = rest[n_cache]
    kv_refs = rest[n_cache + 1:]
    x = x_ref[...]
    shift = mod_ref[:, 0:D_MODEL]
    scale = mod_ref[:, D_MODEL:2 * D_MODEL]
    ms = jnp.mean(x * x, axis=-1, keepdims=True)
    h = (x * lax.rsqrt(ms + EPS)) * g_ref[...]
    h = (h * (1.0 + scale) + shift).astype(BF16)
    for c0 in range(0, IN_WIDTH, PROJ_CHUNK):
        acc = _dot(h, w_ref[:, c0:c0 + PROJ_CHUNK])
        if kv_refs and c0 in (K_OFF, V_OFF):
            kv_ref = kv_refs[(c0 - K_OFF) // KV_WIDTH]
            seq = kv_ref.shape[-2] // N_KV_HEADS
            for s in range(kv_ref.shape[0]):
                for slot in (range(kv_ref.shape[1]) if kv_ref.ndim == 4 else (None,)):
                    dst = kv_ref.at[s] if slot is None else kv_ref.at[s, slot]
                    if slot not in (None, cache_slot):
                        dst[...] = jnp.zeros(dst.shape, F32)
                        continue
                    for head in range(N_KV_HEADS):
                        dst[pl.ds(head, seq, stride=N_KV_HEADS), :] = (
                            acc[s * seq:(s + 1) * seq, head * HEAD_DIM:(head + 1) * HEAD_DIM])
        if q_scale is not None and c0 < ATTN_WIDTH:
            acc = acc * q_scale
        proj_ref[:, c0:c0 + PROJ_CHUNK] = acc.astype(BF16)


def _inproj(x, mod, norm_pre, w_in_bf, layer, q_scale=None, cache=None, cache_shape=None):
    assert KV_WIDTH == PROJ_CHUNK and K_OFF % PROJ_CHUNK == 0 and ATTN_WIDTH % PROJ_CHUNK == 0
    m = x.shape[0]
    n_seq = mod.shape[1]
    tiles_per_seq = m // n_seq // TOKEN_TILE
    in_specs = [
        pl.BlockSpec((TOKEN_TILE, D_MODEL), lambda i: (i, 0)),
        pl.BlockSpec((None, None, 1, 3 * D_MODEL), lambda i: (layer, i // tiles_per_seq, 0, 0)),
        pl.BlockSpec((None, 1, D_MODEL), lambda i: (layer, 0, 0)),
        pl.BlockSpec((None, D_MODEL, IN_WIDTH), lambda i: (layer, 0, 0),
                     pipeline_mode=pl.Buffered(1)),
    ]
    args = [x, mod, norm_pre.reshape(DEPTH, 1, D_MODEL), w_in_bf]
    out_shape = [jax.ShapeDtypeStruct((m, IN_WIDTH), BF16)]
    out_specs = [pl.BlockSpec((TOKEN_TILE, IN_WIDTH), lambda i: (i, 0))]
    aliases = {}
    n_cache = 0
    if cache_shape is not None:
        n_slots, seq_rows = cache_shape[1:3]
        seqs_per_tile = TOKEN_TILE * N_KV_HEADS // seq_rows
        slot = layer // 2
        if cache is None:
            kv_block = pl.BlockSpec((seqs_per_tile, n_slots, seq_rows, HEAD_DIM),
                                    lambda i: (i, 0, 0, 0))
        else:
            kv_block = pl.BlockSpec((seqs_per_tile, None, seq_rows, HEAD_DIM),
                                    lambda i: (i, slot, 0, 0))
        out_shape += [jax.ShapeDtypeStruct(cache_shape, F32)] * 2
        out_specs += [kv_block] * 2
        if cache is not None:
            n_cache = len(cache)
            aliases = {len(args) + t: 1 + t for t in range(n_cache)}
            in_specs += [pl.BlockSpec(memory_space=pl.ANY)] * n_cache
            args += list(cache)
    return pl.pallas_call(
        functools.partial(_inproj_kernel, q_scale=q_scale, n_cache=n_cache,
                          cache_slot=layer // 2),
        grid=(m // TOKEN_TILE,),
        in_specs=in_specs,
        out_specs=out_specs,
        out_shape=out_shape,
        input_output_aliases=aliases,
        compiler_params=_params("arbitrary"),
        name="in_projection",
    )(*args)


def _mixer_call(mixer, n_tiles, in_specs, args, scratch, x, mod, norm_post, w_out_bf, layer,
                rows, mod_row, name):
    n_in = len(in_specs)

    def mixer_tile(t):
        return jnp.minimum(t, n_tiles - 1)

    def tail_tile(t):
        return jnp.maximum(t - 2, 0)

    def clamp(spec):
        if spec.index_map is None:
            return spec
        return pl.BlockSpec(spec.block_shape, lambda t: spec.index_map(mixer_tile(t)),
                            pipeline_mode=spec.pipeline_mode)

    tail_specs = [
        pl.BlockSpec((None, D_MODEL, D_MODEL), lambda t: (layer, 0, 0),
                     pipeline_mode=pl.Buffered(1)),
        pl.BlockSpec((rows, D_MODEL), lambda t: (tail_tile(t), 0)),
        pl.BlockSpec((None, None, 1, 3 * D_MODEL),
                     lambda t: (layer, mod_row(tail_tile(t)), 0, 0)),
        pl.BlockSpec((None, 1, D_MODEL), lambda t: (layer, 0, 0)),
    ]

    def body(*refs):
        mixer_refs = refs[:n_in]
        w_ref, x_ref, mod_ref, g_ref, out_ref = refs[n_in:n_in + 5]
        o_even, o_odd, acc_even, acc_odd = refs[n_in + 5:n_in + 9]
        extra = refs[n_in + 9:]
        t = pl.program_id(0)

        @pl.when(t == 0)
        def _():
            o_odd[...] = jnp.zeros_like(o_odd)
            acc_odd[...] = jnp.zeros_like(acc_odd)

        def step(o_read, o_write, acc_read, acc_write, with_mixer):
            acc = acc_read[...]
            ms = jnp.mean(acc * acc, axis=-1, keepdims=True)
            y = (acc * lax.rsqrt(ms + EPS)) * g_ref[...]
            out_ref[...] = x_ref[...] + mod_ref[:, 2 * D_MODEL:3 * D_MODEL] * y

            def project(c0):
                acc_write[:, c0:c0 + OUT_CHUNK] = _dot(o_read[...], w_ref[:, c0:c0 + OUT_CHUNK])

            chunks = iter(range(0, D_MODEL, OUT_CHUNK))
            if with_mixer:
                for _ in mixer(*mixer_refs, o_write, *extra, tile=mixer_tile(t)):
                    c0 = next(chunks, None)
                    if c0 is not None:
                        project(c0)
            for c0 in chunks:
                project(c0)

        even = t % 2 == 0
        live = t < n_tiles
        pl.when(live & even)(lambda: step(o_odd, o_even, acc_odd, acc_even, True))
        pl.when(live & ~even)(lambda: step(o_even, o_odd, acc_even, acc_odd, True))
        pl.when(~live & even)(lambda: step(o_odd, o_even, acc_odd, acc_even, False))
        pl.when(~live & ~even)(lambda: step(o_even, o_odd, acc_even, acc_odd, False))

    return pl.pallas_call(
        body,
        grid=(n_tiles + 2,),
        in_specs=[clamp(s) for s in in_specs] + tail_specs,
        out_specs=pl.BlockSpec((rows, D_MODEL), lambda t: (tail_tile(t), 0)),
        out_shape=jax.ShapeDtypeStruct(x.shape, F32),
        scratch_shapes=([pltpu.VMEM((rows, D_MODEL), BF16)] * 2
                        + [pltpu.VMEM((rows, D_MODEL), F32)] * 2 + list(scratch)),
        compiler_params=_params("arbitrary"),
        name=name,
    )(*args, w_out_bf, x, mod, norm_post.reshape(DEPTH, 1, D_MODEL))


def _softmax_pv(scores, sink, values):
    m = sink
    for s in scores:
        m = jnp.maximum(m, jnp.max(s, axis=-1, keepdims=True))
    denom = jnp.exp(sink - m)
    out = None
    for s, v in zip(scores, values):
        e = jnp.exp(s - m)
        denom = denom + jnp.sum(e, axis=-1, keepdims=True)
        pv = _dot(e.astype(BF16), v)
        out = pv if out is None else out + pv
    return out / denom


def _fourier_cols(u_ref, col0, dftc):
    tops, bots = [], []
    for g in range(FOURIER_GROUPS):
        c = col0 + g * FOURIER_GROUP_DIM
        y = _dot(u_ref[:, c:c + FOURIER_GROUP_DIM], dftc)
        tops.append(y[:, :FOURIER_GROUP_DIM].astype(BF16))
        bots.append(y[:, FOURIER_GROUP_DIM:].astype(BF16))
    return jnp.concatenate(tops, axis=1), jnp.concatenate(bots, axis=1)


def _rope(x, cos, sin_signed):
    lane = lax.broadcasted_iota(jnp.int32, x.shape, 1)
    partner = jnp.where((lane % 64) < 32,
                        pltpu.roll(x, HEAD_DIM - 32, 1), pltpu.roll(x, 32, 1))
    return x * cos + partner * sin_signed


def _ctx_even_mixer(sink_ref, p_ref, dftc_ref, dftt_ref, o_ref, *, tile):
    del tile
    scores = []
    for hq in range(N_Q_HEADS):
        h = hq // Q_PER_KV
        k = p_ref[:, K_OFF + h * HEAD_DIM:K_OFF + (h + 1) * HEAD_DIM]
        scores.append(_dot_nt(p_ref[:, hq * HEAD_DIM:(hq + 1) * HEAD_DIM], k))
    top, bot = _fourier_cols(p_ref, UB_OFF, dftc_ref[...])
    f = _dot(dftt_ref[...], jnp.concatenate([top, bot], axis=0))
    yield
    for hq in range(N_Q_HEADS):
        h = hq // Q_PER_KV
        v = p_ref[:, V_OFF + h * HEAD_DIM:V_OFF + (h + 1) * HEAD_DIM]
        o = _softmax_pv([scores[hq]], sink_ref[hq], [v])
        ga = p_ref[:, GA_OFF + hq * HEAD_DIM:GA_OFF + (hq + 1) * HEAD_DIM].astype(F32)
        o_ref[:, hq * HEAD_DIM:(hq + 1) * HEAD_DIM] = (_silu(ga) * o).astype(BF16)
        yield
    gb = p_ref[:, GB_OFF:GB_OFF + FOURIER_WIDTH].astype(F32)
    o_ref[:, ATTN_WIDTH:] = (_silu(gb) * f).astype(BF16)
    yield


def _ctx_even(proj, sink, dftc, dftt, seq, **tail):
    m = proj.shape[0]
    return _mixer_call(
        _ctx_even_mixer, m // seq,
        [pl.BlockSpec(memory_space=pltpu.SMEM),
         pl.BlockSpec((seq, IN_WIDTH), lambda b: (b, 0)),
         pl.BlockSpec(dftc.shape, lambda b: (0, 0)),
         pl.BlockSpec(dftt.shape, lambda b: (0, 0))],
        [sink, proj, dftc, dftt], [],
        rows=seq, mod_row=lambda b: 0, name="context_attention_fourier", **tail)


def _lat_even_mixer(sink_ref, row_ref, k_ref, v_ref, kc_ref, vc_ref, cos_ref, sin_ref,
                    u_ref, dftc_ref, dftt_ref, o_ref, kr_ref, z_ref, *, tile, seq):
    i = tile % (seq // BLOCK)
    rope_rows = 256

    @pl.when(i == 0)
    def _():
        def body(c, carry):
            r = pl.multiple_of(c * rope_rows, rope_rows)
            cos = cos_ref[pl.ds(r, rope_rows), :]
            sin = sin_ref[pl.ds(r, rope_rows), :]
            for h in range(N_KV_HEADS):
                cols = slice(h * HEAD_DIM, (h + 1) * HEAD_DIM)
                kr_ref[pl.ds(r, rope_rows), cols] = _rope(
                    k_ref[pl.ds(r, rope_rows), cols].astype(F32), cos, sin).astype(BF16)
            return carry
        lax.fori_loop(0, seq // rope_rows, body, 0)
        top, bot = _fourier_cols(u_ref, 0, dftc_ref[...])
        z_ref[0:seq, :] = top
        z_ref[seq:2 * seq, :] = bot

    span = 3 * BLOCK
    q0 = pl.multiple_of(i * BLOCK, BLOCK)
    start = pl.multiple_of(jnp.clip(i * BLOCK - BLOCK, 0, seq - span), BLOCK)
    cos = cos_ref[pl.ds(q0, BLOCK), :]
    sin = sin_ref[pl.ds(q0, BLOCK), :]
    dist = ((q0 - start) + lax.broadcasted_iota(jnp.int32, (BLOCK, span), 0)
            - lax.broadcasted_iota(jnp.int32, (BLOCK, span), 1))
    valid = (dist >= -WINDOW) & (dist <= WINDOW)
    valid = jnp.concatenate([valid] * Q_PER_KV, axis=0)
    scores = []
    for h in range(N_KV_HEADS):
        kv_cols = slice(h * HEAD_DIM, (h + 1) * HEAD_DIM)
        q = jnp.concatenate(
            [_rope(row_ref[:, (h * Q_PER_KV + g) * HEAD_DIM:(h * Q_PER_KV + g + 1) * HEAD_DIM]
                   .astype(F32), cos, sin).astype(BF16) for g in range(Q_PER_KV)], axis=0)
        s_loc = _dot_nt(q, kr_ref[pl.ds(start, span), kv_cols])
        scores.append((jnp.where(valid, s_loc, NEG_INF), _dot_nt(q, kc_ref[:, kv_cols])))
    f = _dot(dftt_ref[...], z_ref[...])
    yield
    for h in range(N_KV_HEADS):
        kv_cols = slice(h * HEAD_DIM, (h + 1) * HEAD_DIM)
        sink = jnp.concatenate(
            [jnp.full((BLOCK, 1), sink_ref[h * Q_PER_KV + g], F32) for g in range(Q_PER_KV)],
            axis=0)
        o = _softmax_pv(list(scores[h]), sink,
                        [v_ref[pl.ds(start, span), kv_cols], vc_ref[:, kv_cols]])
        yield
        for g in range(Q_PER_KV):
            hq = h * Q_PER_KV + g
            ga = row_ref[:, GA_OFF + hq * HEAD_DIM:GA_OFF + (hq + 1) * HEAD_DIM].astype(F32)
            o_ref[:, hq * HEAD_DIM:(hq + 1) * HEAD_DIM] = (
                _silu(ga) * o[g * BLOCK:(g + 1) * BLOCK]).astype(BF16)
        yield
    gb = row_ref[:, GB_OFF:GB_OFF + FOURIER_WIDTH].astype(F32)
    o_ref[:, ATTN_WIDTH:] = (_silu(gb) * f).astype(BF16)
    yield


def _lat_even(proj, sink, kc, vc, cos, sin, dftc, dftt, seq, **tail):
    m = proj.shape[0]
    nb = seq // BLOCK
    return _mixer_call(
        functools.partial(_lat_even_mixer, seq=seq), m // BLOCK,
        [pl.BlockSpec(memory_space=pltpu.SMEM),
         pl.BlockSpec((BLOCK, IN_WIDTH), lambda t: (t, 0)),
         pl.BlockSpec((seq, KV_WIDTH), lambda t: (t // nb, K_OFF // KV_WIDTH)),
         pl.BlockSpec((seq, KV_WIDTH), lambda t: (t // nb, V_OFF // KV_WIDTH)),
         pl.BlockSpec((None,) + kc.shape[1:], lambda t: (t // nb, 0, 0)),
         pl.BlockSpec((None,) + vc.shape[1:], lambda t: (t // nb, 0, 0)),
         pl.BlockSpec(cos.shape, lambda t: (0, 0)),
         pl.BlockSpec(sin.shape, lambda t: (0, 0)),
         pl.BlockSpec((seq, FOURIER_WIDTH), lambda t: (t // nb, UB_OFF // FOURIER_WIDTH)),
         pl.BlockSpec(dftc.shape, lambda t: (0, 0)),
         pl.BlockSpec((BLOCK, 2 * seq), lambda t: (t % nb, 0))],
        [sink, proj, proj, proj, kc, vc, cos, sin, proj, dftc, dftt],
        [pltpu.VMEM((seq, KV_WIDTH), BF16), pltpu.VMEM((2 * seq, FOURIER_WIDTH), BF16)],
        rows=BLOCK, mod_row=lambda t: t // nb, name="latent_attention_fourier", **tail)


def _odd_mixer_body(uc_ref, gc_ref, ud_ref, vd_ref, gd_ref, pw_ref, ps_ref, sn_ref, sw_ref,
                    sb_ref, o_ref, *, tile, seq):
    r = tile % (seq // SEQ_TILE)
    span = min(2 * SEQ_TILE, seq)
    t0 = pl.multiple_of(r * SEQ_TILE, SEQ_TILE)
    start = pl.multiple_of(jnp.clip(r * SEQ_TILE - SEQ_TILE // 2, 0, seq - span), SEQ_TILE // 2)
    rel = (lax.broadcasted_iota(jnp.int32, (SEQ_TILE, span), 1) + (start - t0)
           - lax.broadcasted_iota(jnp.int32, (SEQ_TILE, span), 0))
    pos = t0 + lax.broadcasted_iota(jnp.int32, (SEQ_TILE, 1), 0)
    for g, w in enumerate(POOL_WINDOWS):
        cols = slice(g * POOL_GROUP_DIM, (g + 1) * POOL_GROUP_DIM)
        band = jnp.where((rel >= -(w // 2)) & (rel < w // 2), 1.0, 0.0).astype(BF16)
        total = _dot(band, uc_ref[pl.ds(start, span), cols])
        cnt = (jnp.minimum(pos + w // 2, seq) - jnp.maximum(pos - w // 2, 0)).astype(F32)
        pooled = total / cnt - uc_ref[pl.ds(t0, SEQ_TILE), cols].astype(F32)
        mixed = _dot(pooled.astype(BF16), pw_ref[g]) * ps_ref[:, cols]
        o_ref[:, cols] = (_silu(gc_ref[:, cols].astype(F32)) * mixed).astype(BF16)
        yield
    for h in range(SGU_HEADS):
        cols = slice(h * SGU_GROUP_DIM, (h + 1) * SGU_GROUP_DIM)
        v = vd_ref[:, cols].astype(F32)
        mu = jnp.mean(v, axis=-1, keepdims=True)
        vc = v - mu
        var = jnp.mean(vc * vc, axis=-1, keepdims=True)
        vn = (vc * lax.rsqrt(var + EPS) * sn_ref[:, cols]).astype(BF16)
        bias = sb_ref[:, h:h + 1]
        for c in range(SEQ_TILE // SGU_CHUNK):
            rows = slice(c * SGU_CHUNK, (c + 1) * SGU_CHUNK)
            gated = ud_ref[rows, cols].astype(F32) * (_dot(sw_ref[h], vn[rows]) + bias)
            o_ref[rows, POOL_WIDTH + h * SGU_GROUP_DIM:POOL_WIDTH + (h + 1) * SGU_GROUP_DIM] = (
                _silu(gd_ref[rows, cols].astype(F32)) * gated).astype(BF16)
        yield


def _odd_mixer(proj, pool_w_bf, pool_scale, sgu_norm, sgu_w_bf, sgu_b_t, j, seq, n_mod, **tail):
    m = proj.shape[0]
    nt = seq // SEQ_TILE

    def tile_spec(off):
        return pl.BlockSpec((SEQ_TILE, POOL_WIDTH), lambda t: (t, off // POOL_WIDTH))

    return _mixer_call(
        functools.partial(_odd_mixer_body, seq=seq), m // SEQ_TILE,
        [pl.BlockSpec((seq, POOL_WIDTH), lambda t: (t // nt, UC_OFF // POOL_WIDTH)),
         tile_spec(GC_OFF), tile_spec(UD_OFF), tile_spec(VD_OFF), tile_spec(GD_OFF),
         pl.BlockSpec((None,) + pool_w_bf.shape[1:], lambda t: (j, 0, 0, 0)),
         pl.BlockSpec((None, 1, POOL_WIDTH), lambda t: (j, 0, 0)),
         pl.BlockSpec((None, 1, SGU_WIDTH), lambda t: (j, 0, 0)),
         pl.BlockSpec((None,) + sgu_w_bf.shape[1:], lambda t: (j, 0, 0, 0)),
         pl.BlockSpec((None,) + sgu_b_t.shape[1:], lambda t: (j, 0, 0))],
        [proj, proj, proj, proj, proj, pool_w_bf, pool_scale.reshape(-1, 1, POOL_WIDTH),
         sgu_norm.reshape(-1, 1, SGU_WIDTH), sgu_w_bf, sgu_b_t], [],
        rows=SEQ_TILE, mod_row=(lambda t: t // nt) if n_mod > 1 else (lambda t: 0),
        name="pool_spatial_gating", **tail)


def _dft_tables(seq):
    def cos_sin(n):
        idx = np.arange(n)
        ang = 2.0 * np.pi * ((idx[:, None] * idx[None, :]) % n) / n
        return np.cos(ang) / np.sqrt(n), np.sin(ang) / np.sqrt(n)

    cc, sc = cos_sin(FOURIER_GROUP_DIM)
    ct, st = cos_sin(seq)
    dftc = jnp.asarray(np.concatenate([cc, sc], axis=1), dtype=F32).astype(BF16)
    dftt = jnp.asarray(np.concatenate([ct, -st], axis=1), dtype=F32).astype(BF16)
    return dftc, dftt


def _rope_tables(seq):
    n_freq = HEAD_DIM // 4
    t = np.arange(seq)
    inv = ROPE_THETA ** (-np.arange(n_freq, dtype=np.float64) / n_freq)
    ang = np.stack([(t // GRID_W)[:, None] * inv, (t % GRID_W)[:, None] * inv], axis=1)
    cos = np.repeat(np.cos(ang)[:, :, None, :], 2, axis=2).reshape(seq, HEAD_DIM)
    sin = np.sin(ang)
    sin = np.stack([-sin, sin], axis=2).reshape(seq, HEAD_DIM)
    return jnp.asarray(cos, dtype=F32), jnp.asarray(sin, dtype=F32)


def kernel(x_prompt, x_sample, cache_k, cache_v, c, c_ctx, w_ada, b_ada, norm_pre, norm_post,
           w_in, w_out, attn_sink, pool_w, pool_scale, sgu_norm, sgu_w, sgu_b):
    batch, seq, _ = x_prompt.shape
    dec_batch, dec_seq, _ = x_sample.shape
    n_attn, past = cache_k.shape[1], cache_k.shape[2]

    w_in_bf = w_in.astype(BF16)
    w_out_bf = w_out.astype(BF16)
    pool_w_bf = pool_w.astype(BF16)
    sgu_w_bf = sgu_w.astype(BF16)
    sgu_b_t = jnp.swapaxes(sgu_b, 1, 2)
    kc = cache_k.reshape(dec_batch, n_attn, past, KV_WIDTH).astype(BF16)
    vc = cache_v.reshape(dec_batch, n_attn, past, KV_WIDTH).astype(BF16)

    conds = jnp.concatenate(
        [c_ctx[None, :], c, jnp.zeros((COND_ROWS - 1 - dec_batch, D_MODEL), F32)], axis=0)
    mod = _ada(conds, w_ada, b_ada)
    mod_p = mod[:, 0:1].reshape(DEPTH, 1, 1, 3 * D_MODEL)
    mod_s = mod[:, 1:1 + dec_batch].reshape(DEPTH, dec_batch, 1, 3 * D_MODEL)

    dftc, dftt_p = _dft_tables(seq)
    _, dftt_s = _dft_tables(dec_seq)
    cos, sin = _rope_tables(dec_seq)

    xp = x_prompt.reshape(batch * seq, D_MODEL)
    xs = x_sample.reshape(dec_batch * dec_seq, D_MODEL)
    cache = None
    cache_shape = (batch, n_attn, seq * N_KV_HEADS, HEAD_DIM)
    for l in range(DEPTH):
        j = l // 2
        tail_p = dict(x=xp, mod=mod_p, norm_post=norm_post, w_out_bf=w_out_bf, layer=l)
        tail_s = dict(x=xs, mod=mod_s, norm_post=norm_post, w_out_bf=w_out_bf, layer=l)
        if l % 2 == 0:
            proj_p, *cache = _inproj(xp, mod_p, norm_pre, w_in_bf, l, q_scale=ATTN_SCALE,
                                     cache=cache, cache_shape=cache_shape)
            proj_s, = _inproj(xs, mod_s, norm_pre, w_in_bf, l, q_scale=ATTN_SCALE)
            xp = _ctx_even(proj_p, attn_sink[j], dftc, dftt_p, seq, **tail_p)
            xs = _lat_even(proj_s, attn_sink[j], kc[:, j], vc[:, j], cos, sin, dftc, dftt_s,
                           dec_seq, **tail_s)
        else:
            proj_p, = _inproj(xp, mod_p, norm_pre, w_in_bf, l)
            proj_s, = _inproj(xs, mod_s, norm_pre, w_in_bf, l)
            odd = (pool_w_bf, pool_scale, sgu_norm, sgu_w_bf, sgu_b_t, j)
            xp = _odd_mixer(proj_p, *odd, seq, 1, **tail_p)
            xs = _odd_mixer(proj_s, *odd, dec_seq, dec_batch, **tail_s)

    new_k, new_v = (t.reshape(batch, n_attn, seq, N_KV_HEADS, HEAD_DIM) for t in cache)
    return (xp.reshape(batch, seq, D_MODEL), xs.reshape(dec_batch, dec_seq, D_MODEL), new_k, new_v)
```

```python
import functools

import numpy as np
import jax
import jax.numpy as jnp
from jax import lax
from jax.experimental import pallas as pl
from jax.experimental.pallas import tpu as pltpu

F32 = jnp.float32
BF16 = jnp.bfloat16

D_MODEL = 2048
DEPTH = 4
GRID_W = 64
HEAD_DIM = 128
N_Q_HEADS = 12
N_KV_HEADS = 4
Q_PER_KV = N_Q_HEADS // N_KV_HEADS
ATTN_WIDTH = N_Q_HEADS * HEAD_DIM
KV_WIDTH = N_KV_HEADS * HEAD_DIM
WINDOW = 128
BLOCK = 128
ROPE_THETA = 10000.0
FOURIER_GROUPS = 4
FOURIER_WIDTH = D_MODEL // 4
FOURIER_GROUP_DIM = FOURIER_WIDTH // FOURIER_GROUPS
POOL_WINDOWS = (2, 4, 8, 16)
POOL_WIDTH = D_MODEL // 2
POOL_GROUP_DIM = POOL_WIDTH // len(POOL_WINDOWS)
SGU_HEADS = 4
SGU_WIDTH = D_MODEL // 2
SGU_GROUP_DIM = SGU_WIDTH // SGU_HEADS
SGU_CHUNK = 128
IN_WIDTH = 2 * ATTN_WIDTH + 2 * KV_WIDTH + 2 * FOURIER_WIDTH
EPS = 1e-6
NEG_INF = -1e30
ATTN_SCALE = HEAD_DIM ** -0.5

Q_OFF, K_OFF, V_OFF = 0, ATTN_WIDTH, ATTN_WIDTH + KV_WIDTH
GA_OFF = ATTN_WIDTH + 2 * KV_WIDTH
UB_OFF = GA_OFF + ATTN_WIDTH
GB_OFF = UB_OFF + FOURIER_WIDTH
UC_OFF, GC_OFF, UD_OFF, VD_OFF, GD_OFF = (i * POOL_WIDTH for i in range(5))

V7X_VMEM_BYTES = 64 * 1024 * 1024
VMEM_LIMIT = V7X_VMEM_BYTES - 8 * 1024 * 1024

TOKEN_TILE = 512
PROJ_CHUNK = 512
SEQ_TILE = 256
ADA_TILE = 1024
OUT_CHUNK = 256
COND_ROWS = 8


def _silu(x):
    return x / (1.0 + jnp.exp(-x))


def _params(*sem):
    return pltpu.CompilerParams(dimension_semantics=sem, vmem_limit_bytes=VMEM_LIMIT)


def _dot(a, b):
    return jnp.dot(a, b, preferred_element_type=F32)


def _dot_nt(a, b):
    return lax.dot_general(a, b, (((1,), (1,)), ((), ())), preferred_element_type=F32)


def _ada_kernel(c_ref, w_ref, b_ref, o_ref):
    s = _silu(c_ref[...]).astype(BF16)
    o_ref[...] = _dot(s, w_ref[...].astype(BF16)) + b_ref[...]


def _ada(conds, w_ada, b_ada):
    n = w_ada.shape[-1]
    return pl.pallas_call(
        _ada_kernel,
        grid=(DEPTH, n // ADA_TILE),
        in_specs=[
            pl.BlockSpec((COND_ROWS, D_MODEL), lambda l, j: (0, 0)),
            pl.BlockSpec((None, D_MODEL, ADA_TILE), lambda l, j: (l, 0, j)),
            pl.BlockSpec((None, 1, ADA_TILE), lambda l, j: (l, 0, j)),
        ],
        out_specs=pl.BlockSpec((None, COND_ROWS, ADA_TILE), lambda l, j: (l, 0, j)),
        out_shape=jax.ShapeDtypeStruct((DEPTH, COND_ROWS, n), F32),
        compiler_params=_params("arbitrary", "arbitrary"),
        name="ada_modulation",
    )(conds, w_ada, b_ada.reshape(DEPTH, 1, n))


def _inproj_kernel(x_ref, mod_ref, g_ref, w_ref, *rest, q_scale, n_cache, cache_slot, n_cast):
    outs = rest[n_cache + n_cast:]
    if n_cast:
        outs[-1][...] = rest[n_cache][...].astype(BF16)
        outs = outs[:-1]
    proj_ref, kv_refs = outs[0], outs[1:]
    x = x_ref[...]
    shift = mod_ref[:, 0:D_MODEL]
    scale = mod_ref[:, D_MODEL:2 * D_MODEL]
    ms = jnp.mean(x * x, axis=-1, keepdims=True)
    h = (x * lax.rsqrt(ms + EPS)) * g_ref[...]
    h = (h * (1.0 + scale) + shift).astype(BF16)
    for c0 in range(0, IN_WIDTH, PROJ_CHUNK):
        acc = _dot(h, w_ref[:, c0:c0 + PROJ_CHUNK])
        if kv_refs and c0 in (K_OFF, V_OFF):
            kv_ref = kv_refs[(c0 - K_OFF) // KV_WIDTH]
            seq = kv_ref.shape[-2] // N_KV_HEADS
            for s in range(kv_ref.shape[0]):
                for slot in (range(kv_ref.shape[1]) if kv_ref.ndim == 4 else (None,)):
                    dst = kv_ref.at[s] if slot is None else kv_ref.at[s, slot]
                    if slot not in (None, cache_slot):
                        dst[...] = jnp.zeros(dst.shape, F32)
                        continue
                    for head in range(N_KV_HEADS):
                        dst[pl.ds(head, seq, stride=N_KV_HEADS), :] = (
                            acc[s * seq:(s + 1) * seq, head * HEAD_DIM:(head + 1) * HEAD_DIM])
        if q_scale is not None and c0 < ATTN_WIDTH:
            acc = acc * q_scale
        proj_ref[:, c0:c0 + PROJ_CHUNK] = acc.astype(BF16)


def _inproj(x, mod, norm_pre, w_in_bf, layer, q_scale=None, cache=None, cache_shape=None,
            cast=None):
    assert KV_WIDTH == PROJ_CHUNK and K_OFF % PROJ_CHUNK == 0 and ATTN_WIDTH % PROJ_CHUNK == 0
    m = x.shape[0]
    n_seq = mod.shape[1]
    n_steps = m // TOKEN_TILE
    tiles_per_seq = n_steps // n_seq
    in_specs = [
        pl.BlockSpec((TOKEN_TILE, D_MODEL), lambda i: (i, 0)),
        pl.BlockSpec((None, None, 1, 3 * D_MODEL), lambda i: (layer, i // tiles_per_seq, 0, 0)),
        pl.BlockSpec((None, 1, D_MODEL), lambda i: (layer, 0, 0)),
        pl.BlockSpec(w_in_bf.shape, lambda i: (0, 0), pipeline_mode=pl.Buffered(1)),
    ]
    args = [x, mod, norm_pre.reshape(DEPTH, 1, D_MODEL), w_in_bf]
    out_shape = [jax.ShapeDtypeStruct((m, IN_WIDTH), BF16)]
    out_specs = [pl.BlockSpec((TOKEN_TILE, IN_WIDTH), lambda i: (i, 0))]
    aliases = {}
    n_cache = 0
    if cache_shape is not None:
        n_slots, seq_rows = cache_shape[1:3]
        seqs_per_tile = TOKEN_TILE * N_KV_HEADS // seq_rows
        slot = layer // 2
        if cache is None:
            kv_block = pl.BlockSpec((seqs_per_tile, n_slots, seq_rows, HEAD_DIM),
                                    lambda i: (i, 0, 0, 0))
        else:
            kv_block = pl.BlockSpec((seqs_per_tile, None, seq_rows, HEAD_DIM),
                                    lambda i: (i, slot, 0, 0))
        out_shape += [jax.ShapeDtypeStruct(cache_shape, F32)] * 2
        out_specs += [kv_block] * 2
        if cache is not None:
            n_cache = len(cache)
            aliases = {len(args) + t: 1 + t for t in range(n_cache)}
            in_specs += [pl.BlockSpec(memory_space=pl.ANY)] * n_cache
            args += list(cache)
    if cast is not None:
        weights, index = cast
        rows, cols = weights.shape[1:]
        in_specs.append(pl.BlockSpec((None, rows // n_steps, cols), lambda i: (index, i, 0)))
        args.append(weights)
        out_shape.append(jax.ShapeDtypeStruct((rows, cols), BF16))
        out_specs.append(pl.BlockSpec((rows // n_steps, cols), lambda i: (i, 0)))
    return pl.pallas_call(
        functools.partial(_inproj_kernel, q_scale=q_scale, n_cache=n_cache,
                          cache_slot=layer // 2, n_cast=int(cast is not None)),
        grid=(n_steps,),
        in_specs=in_specs,
        out_specs=out_specs,
        out_shape=out_shape,
        input_output_aliases=aliases,
        compiler_params=_params("arbitrary"),
        name="in_projection",
    )(*args)


def _mixer_call(mixer, n_tiles, in_specs, args, scratch, x, mod, norm_post, w_out_bf, layer,
                rows, mod_row, name):
    n_in = len(in_specs)

    def mixer_tile(t):
        return jnp.minimum(t, n_tiles - 1)

    def tail_tile(t):
        return jnp.maximum(t - 2, 0)

    def clamp(spec):
        if spec.index_map is None:
            return spec
        return pl.BlockSpec(spec.block_shape, lambda t: spec.index_map(mixer_tile(t)),
                            pipeline_mode=spec.pipeline_mode)

    tail_specs = [
        pl.BlockSpec(w_out_bf.shape, lambda t: (0, 0), pipeline_mode=pl.Buffered(1)),
        pl.BlockSpec((rows, D_MODEL), lambda t: (tail_tile(t), 0)),
        pl.BlockSpec((None, None, 1, 3 * D_MODEL),
                     lambda t: (layer, mod_row(tail_tile(t)), 0, 0)),
        pl.BlockSpec((None, 1, D_MODEL), lambda t: (layer, 0, 0)),
    ]

    def body(*refs):
        mixer_refs = refs[:n_in]
        w_ref, x_ref, mod_ref, g_ref, out_ref = refs[n_in:n_in + 5]
        o_even, o_odd, acc_even, acc_odd = refs[n_in + 5:n_in + 9]
        extra = refs[n_in + 9:]
        t = pl.program_id(0)

        @pl.when(t == 0)
        def _():
            o_odd[...] = jnp.zeros_like(o_odd)
            acc_odd[...] = jnp.zeros_like(acc_odd)

        def step(o_read, o_write, acc_read, acc_write, with_mixer):
            acc = acc_read[...]
            ms = jnp.mean(acc * acc, axis=-1, keepdims=True)
            y = (acc * lax.rsqrt(ms + EPS)) * g_ref[...]
            out_ref[...] = x_ref[...] + mod_ref[:, 2 * D_MODEL:3 * D_MODEL] * y

            def project(c0):
                acc_write[:, c0:c0 + OUT_CHUNK] = _dot(o_read[...], w_ref[:, c0:c0 + OUT_CHUNK])

            chunks = iter(range(0, D_MODEL, OUT_CHUNK))
            if with_mixer:
                for _ in mixer(*mixer_refs, o_write, *extra, tile=mixer_tile(t)):
                    c0 = next(chunks, None)
                    if c0 is not None:
                        project(c0)
            for c0 in chunks:
                project(c0)

        even = t % 2 == 0
        live = t < n_tiles
        pl.when(live & even)(lambda: step(o_odd, o_even, acc_odd, acc_even, True))
        pl.when(live & ~even)(lambda: step(o_even, o_odd, acc_even, acc_odd, True))
        pl.when(~live & even)(lambda: step(o_odd, o_even, acc_odd, acc_even, False))
        pl.when(~live & ~even)(lambda: step(o_even, o_odd, acc_even, acc_odd, False))

    return pl.pallas_call(
        body,
        grid=(n_tiles + 2,),
        in_specs=[clamp(s) for s in in_specs] + tail_specs,
        out_specs=pl.BlockSpec((rows, D_MODEL), lambda t: (tail_tile(t), 0)),
        out_shape=jax.ShapeDtypeStruct(x.shape, F32),
        scratch_shapes=([pltpu.VMEM((rows, D_MODEL), BF16)] * 2
                        + [pltpu.VMEM((rows, D_MODEL), F32)] * 2 + list(scratch)),
        compiler_params=_params("arbitrary"),
        name=name,
    )(*args, w_out_bf, x, mod, norm_post.reshape(DEPTH, 1, D_MODEL))


def _softmax_pv(scores, sink, values):
    m = sink
    for s in scores:
        m = jnp.maximum(m, jnp.max(s, axis=-1, keepdims=True))
    denom = jnp.exp(sink - m)
    out = None
    for s, v in zip(scores, values):
        e = jnp.exp(s - m)
        denom = denom + jnp.sum(e, axis=-1, keepdims=True)
        pv = _dot(e.astype(BF16), v)
        out = pv if out is None else out + pv
    return out / denom


def _fourier_cols(u_ref, col0, dftc):
    tops, bots = [], []
    for g in range(FOURIER_GROUPS):
        c = col0 + g * FOURIER_GROUP_DIM
        y = _dot(u_ref[:, c:c + FOURIER_GROUP_DIM], dftc)
        tops.append(y[:, :FOURIER_GROUP_DIM].astype(BF16))
        bots.append(y[:, FOURIER_GROUP_DIM:].astype(BF16))
    return jnp.concatenate(tops, axis=1), jnp.concatenate(bots, axis=1)


def _rope(x, cos, sin_signed):
    lane = lax.broadcasted_iota(jnp.int32, x.shape, 1)
    partner = jnp.where((lane % 64) < 32,
                        pltpu.roll(x, HEAD_DIM - 32, 1), pltpu.roll(x, 32, 1))
    return x * cos + partner * sin_signed


def _ctx_even_mixer(sink_ref, p_ref, dftc_ref, dftt_ref, o_ref, *, tile):
    del tile
    scores = []
    for hq in range(N_Q_HEADS):
        h = hq // Q_PER_KV
        k = p_ref[:, K_OFF + h * HEAD_DIM:K_OFF + (h + 1) * HEAD_DIM]
        scores.append(_dot_nt(p_ref[:, hq * HEAD_DIM:(hq + 1) * HEAD_DIM], k))
    top, bot = _fourier_cols(p_ref, UB_OFF, dftc_ref[...])
    f = _dot(dftt_ref[...], jnp.concatenate([top, bot], axis=0))
    yield
    for hq in range(N_Q_HEADS):
        h = hq // Q_PER_KV
        v = p_ref[:, V_OFF + h * HEAD_DIM:V_OFF + (h + 1) * HEAD_DIM]
        o = _softmax_pv([scores[hq]], sink_ref[hq], [v])
        ga = p_ref[:, GA_OFF + hq * HEAD_DIM:GA_OFF + (hq + 1) * HEAD_DIM].astype(F32)
        o_ref[:, hq * HEAD_DIM:(hq + 1) * HEAD_DIM] = (_silu(ga) * o).astype(BF16)
        yield
    gb = p_ref[:, GB_OFF:GB_OFF + FOURIER_WIDTH].astype(F32)
    o_ref[:, ATTN_WIDTH:] = (_silu(gb) * f).astype(BF16)
    yield


def _ctx_even(proj, sink, dftc, dftt, seq, **tail):
    m = proj.shape[0]
    return _mixer_call(
        _ctx_even_mixer, m // seq,
        [pl.BlockSpec(memory_space=pltpu.SMEM),
         pl.BlockSpec((seq, IN_WIDTH), lambda b: (b, 0)),
         pl.BlockSpec(dftc.shape, lambda b: (0, 0)),
         pl.BlockSpec(dftt.shape, lambda b: (0, 0))],
        [sink, proj, dftc, dftt], [],
        rows=seq, mod_row=lambda b: 0, name="context_attention_fourier", **tail)


def _lat_even_mixer(sink_ref, row_ref, k_ref, v_ref, kc_ref, vc_ref, cos_ref, sin_ref,
                    u_ref, dftc_ref, dftt_ref, o_ref, kr_ref, z_ref, *, tile, seq):
    i = tile % (seq // BLOCK)
    rope_rows = 256

    @pl.when(i == 0)
    def _():
        def body(c, carry):
            r = pl.multiple_of(c * rope_rows, rope_rows)
            cos = cos_ref[pl.ds(r, rope_rows), :]
            sin = sin_ref[pl.ds(r, rope_rows), :]
            for h in range(N_KV_HEADS):
                cols = slice(h * HEAD_DIM, (h + 1) * HEAD_DIM)
                kr_ref[pl.ds(r, rope_rows), cols] = _rope(
                    k_ref[pl.ds(r, rope_rows), cols].astype(F32), cos, sin).astype(BF16)
            return carry
        lax.fori_loop(0, seq // rope_rows, body, 0)
        top, bot = _fourier_cols(u_ref, 0, dftc_ref[...])
        z_ref[0:seq, :] = top
        z_ref[seq:2 * seq, :] = bot

    span = 3 * BLOCK
    q0 = pl.multiple_of(i * BLOCK, BLOCK)
    start = pl.multiple_of(jnp.clip(i * BLOCK - BLOCK, 0, seq - span), BLOCK)
    cos = cos_ref[pl.ds(q0, BLOCK), :]
    sin = sin_ref[pl.ds(q0, BLOCK), :]
    dist = ((q0 - start) + lax.broadcasted_iota(jnp.int32, (BLOCK, span), 0)
            - lax.broadcasted_iota(jnp.int32, (BLOCK, span), 1))
    valid = (dist >= -WINDOW) & (dist <= WINDOW)
    valid = jnp.concatenate([valid] * Q_PER_KV, axis=0)
    scores = []
    for h in range(N_KV_HEADS):
        kv_cols = slice(h * HEAD_DIM, (h + 1) * HEAD_DIM)
        q = jnp.concatenate(
            [_rope(row_ref[:, (h * Q_PER_KV + g) * HEAD_DIM:(h * Q_PER_KV + g + 1) * HEAD_DIM]
                   .astype(F32), cos, sin).astype(BF16) for g in range(Q_PER_KV)], axis=0)
        s_loc = _dot_nt(q, kr_ref[pl.ds(start, span), kv_cols])
        scores.append((jnp.where(valid, s_loc, NEG_INF), _dot_nt(q, kc_ref[:, kv_cols])))
    f = _dot(dftt_ref[...], z_ref[...])
    yield
    for h in range(N_KV_HEADS):
        kv_cols = slice(h * HEAD_DIM, (h + 1) * HEAD_DIM)
        sink = jnp.concatenate(
            [jnp.full((BLOCK, 1), sink_ref[h * Q_PER_KV + g], F32) for g in range(Q_PER_KV)],
            axis=0)
        o = _softmax_pv(list(scores[h]), sink,
                        [v_ref[pl.ds(start, span), kv_cols], vc_ref[:, kv_cols]])
        yield
        for g in range(Q_PER_KV):
            hq = h * Q_PER_KV + g
            ga = row_ref[:, GA_OFF + hq * HEAD_DIM:GA_OFF + (hq + 1) * HEAD_DIM].astype(F32)
            o_ref[:, hq * HEAD_DIM:(hq + 1) * HEAD_DIM] = (
                _silu(ga) * o[g * BLOCK:(g + 1) * BLOCK]).astype(BF16)
        yield
    gb = row_ref[:, GB_OFF:GB_OFF + FOURIER_WIDTH].astype(F32)
    o_ref[:, ATTN_WIDTH:] = (_silu(gb) * f).astype(BF16)
    yield


def _lat_even(proj, sink, kc, vc, cos, sin, dftc, dftt, seq, **tail):
    m = proj.shape[0]
    nb = seq // BLOCK
    return _mixer_call(
        functools.partial(_lat_even_mixer, seq=seq), m // BLOCK,
        [pl.BlockSpec(memory_space=pltpu.SMEM),
         pl.BlockSpec((BLOCK, IN_WIDTH), lambda t: (t, 0)),
         pl.BlockSpec((seq, KV_WIDTH), lambda t: (t // nb, K_OFF // KV_WIDTH)),
         pl.BlockSpec((seq, KV_WIDTH), lambda t: (t // nb, V_OFF // KV_WIDTH)),
         pl.BlockSpec((None,) + kc.shape[1:], lambda t: (t // nb, 0, 0)),
         pl.BlockSpec((None,) + vc.shape[1:], lambda t: (t // nb, 0, 0)),
         pl.BlockSpec(cos.shape, lambda t: (0, 0)),
         pl.BlockSpec(sin.shape, lambda t: (0, 0)),
         pl.BlockSpec((seq, FOURIER_WIDTH), lambda t: (t // nb, UB_OFF // FOURIER_WIDTH)),
         pl.BlockSpec(dftc.shape, lambda t: (0, 0)),
         pl.BlockSpec((BLOCK, 2 * seq), lambda t: (t % nb, 0))],
        [sink, proj, proj, proj, kc, vc, cos, sin, proj, dftc, dftt],
        [pltpu.VMEM((seq, KV_WIDTH), BF16), pltpu.VMEM((2 * seq, FOURIER_WIDTH), BF16)],
        rows=BLOCK, mod_row=lambda t: t // nb, name="latent_attention_fourier", **tail)


def _odd_mixer_body(uc_ref, gc_ref, ud_ref, vd_ref, gd_ref, pw_ref, ps_ref, sn_ref, sw_ref,
                    sb_ref, o_ref, *, tile, seq):
    r = tile % (seq // SEQ_TILE)
    span = min(2 * SEQ_TILE, seq)
    t0 = pl.multiple_of(r * SEQ_TILE, SEQ_TILE)
    start = pl.multiple_of(jnp.clip(r * SEQ_TILE - SEQ_TILE // 2, 0, seq - span), SEQ_TILE // 2)
    rel = (lax.broadcasted_iota(jnp.int32, (SEQ_TILE, span), 1) + (start - t0)
           - lax.broadcasted_iota(jnp.int32, (SEQ_TILE, span), 0))
    pos = t0 + lax.broadcasted_iota(jnp.int32, (SEQ_TILE, 1), 0)
    for g, w in enumerate(POOL_WINDOWS):
        cols = slice(g * POOL_GROUP_DIM, (g + 1) * POOL_GROUP_DIM)
        band = jnp.where((rel >= -(w // 2)) & (rel < w // 2), 1.0, 0.0).astype(BF16)
        total = _dot(band, uc_ref[pl.ds(start, span), cols])
        cnt = (jnp.minimum(pos + w // 2, seq) - jnp.maximum(pos - w // 2, 0)).astype(F32)
        pooled = total / cnt - uc_ref[pl.ds(t0, SEQ_TILE), cols].astype(F32)
        mixed = _dot(pooled.astype(BF16), pw_ref[g]) * ps_ref[:, cols]
        o_ref[:, cols] = (_silu(gc_ref[:, cols].astype(F32)) * mixed).astype(BF16)
        yield
    for h in range(SGU_HEADS):
        cols = slice(h * SGU_GROUP_DIM, (h + 1) * SGU_GROUP_DIM)
        v = vd_ref[:, cols].astype(F32)
        mu = jnp.mean(v, axis=-1, keepdims=True)
        vc = v - mu
        var = jnp.mean(vc * vc, axis=-1, keepdims=True)
        vn = (vc * lax.rsqrt(var + EPS) * sn_ref[:, cols]).astype(BF16)
        bias = sb_ref[:, h:h + 1]
        for c in range(SEQ_TILE // SGU_CHUNK):
            rows = slice(c * SGU_CHUNK, (c + 1) * SGU_CHUNK)
            gated = ud_ref[rows, cols].astype(F32) * (_dot(sw_ref[h], vn[rows]) + bias)
            o_ref[rows, POOL_WIDTH + h * SGU_GROUP_DIM:POOL_WIDTH + (h + 1) * SGU_GROUP_DIM] = (
                _silu(gd_ref[rows, cols].astype(F32)) * gated).astype(BF16)
        yield


def _odd_mixer(proj, pool_w_bf, pool_scale, sgu_norm, sgu_w_bf, sgu_b_t, j, seq, n_mod, **tail):
    m = proj.shape[0]
    nt = seq // SEQ_TILE

    def tile_spec(off):
        return pl.BlockSpec((SEQ_TILE, POOL_WIDTH), lambda t: (t, off // POOL_WIDTH))

    return _mixer_call(
        functools.partial(_odd_mixer_body, seq=seq), m // SEQ_TILE,
        [pl.BlockSpec((seq, POOL_WIDTH), lambda t: (t // nt, UC_OFF // POOL_WIDTH)),
         tile_spec(GC_OFF), tile_spec(UD_OFF), tile_spec(VD_OFF), tile_spec(GD_OFF),
         pl.BlockSpec((None,) + pool_w_bf.shape[1:], lambda t: (j, 0, 0, 0)),
         pl.BlockSpec((None, 1, POOL_WIDTH), lambda t: (j, 0, 0)),
         pl.BlockSpec((None, 1, SGU_WIDTH), lambda t: (j, 0, 0)),
         pl.BlockSpec((None,) + sgu_w_bf.shape[1:], lambda t: (j, 0, 0, 0)),
         pl.BlockSpec((None,) + sgu_b_t.shape[1:], lambda t: (j, 0, 0))],
        [proj, proj, proj, proj, proj, pool_w_bf, pool_scale.reshape(-1, 1, POOL_WIDTH),
         sgu_norm.reshape(-1, 1, SGU_WIDTH), sgu_w_bf, sgu_b_t], [],
        rows=SEQ_TILE, mod_row=(lambda t: t // nt) if n_mod > 1 else (lambda t: 0),
        name="pool_spatial_gating", **tail)


def _dft_tables(seq):
    def cos_sin(n):
        idx = np.arange(n)
        ang = 2.0 * np.pi * ((idx[:, None] * idx[None, :]) % n) / n
        return np.cos(ang) / np.sqrt(n), np.sin(ang) / np.sqrt(n)

    cc, sc = cos_sin(FOURIER_GROUP_DIM)
    ct, st = cos_sin(seq)
    dftc = jnp.asarray(np.concatenate([cc, sc], axis=1), dtype=F32).astype(BF16)
    dftt = jnp.asarray(np.concatenate([ct, -st], axis=1), dtype=F32).astype(BF16)
    return dftc, dftt


def _rope_tables(seq):
    n_freq = HEAD_DIM // 4
    t = np.arange(seq)
    inv = ROPE_THETA ** (-np.arange(n_freq, dtype=np.float64) / n_freq)
    ang = np.stack([(t // GRID_W)[:, None] * inv, (t % GRID_W)[:, None] * inv], axis=1)
    cos = np.repeat(np.cos(ang)[:, :, None, :], 2, axis=2).reshape(seq, HEAD_DIM)
    sin = np.sin(ang)
    sin = np.stack([-sin, sin], axis=2).reshape(seq, HEAD_DIM)
    return jnp.asarray(cos, dtype=F32), jnp.asarray(sin, dtype=F32)


def kernel(x_prompt, x_sample, cache_k, cache_v, c, c_ctx, w_ada, b_ada, norm_pre, norm_post,
           w_in, w_out, attn_sink, pool_w, pool_scale, sgu_norm, sgu_w, sgu_b):
    batch, seq, _ = x_prompt.shape
    dec_batch, dec_seq, _ = x_sample.shape
    n_attn, past = cache_k.shape[1], cache_k.shape[2]

    pool_w_bf = pool_w.astype(BF16)
    sgu_w_bf = sgu_w.astype(BF16)
    sgu_b_t = jnp.swapaxes(sgu_b, 1, 2)
    kc = cache_k.reshape(dec_batch, n_attn, past, KV_WIDTH).astype(BF16)
    vc = cache_v.reshape(dec_batch, n_attn, past, KV_WIDTH).astype(BF16)

    conds = jnp.concatenate(
        [c_ctx[None, :], c, jnp.zeros((COND_ROWS - 1 - dec_batch, D_MODEL), F32)], axis=0)
    mod = _ada(conds, w_ada, b_ada)
    mod_p = mod[:, 0:1].reshape(DEPTH, 1, 1, 3 * D_MODEL)
    mod_s = mod[:, 1:1 + dec_batch].reshape(DEPTH, dec_batch, 1, 3 * D_MODEL)

    dftc, dftt_p = _dft_tables(seq)
    _, dftt_s = _dft_tables(dec_seq)
    cos, sin = _rope_tables(dec_seq)

    xp = x_prompt.reshape(batch * seq, D_MODEL)
    xs = x_sample.reshape(dec_batch * dec_seq, D_MODEL)
    cache = None
    cache_shape = (batch, n_attn, seq * N_KV_HEADS, HEAD_DIM)
    w_in_bf = w_in[0].astype(BF16)
    for l in range(DEPTH):
        j = l // 2
        q_scale = ATTN_SCALE if l % 2 == 0 else None
        last = l == DEPTH - 1
        res_p = _inproj(xp, mod_p, norm_pre, w_in_bf, l, q_scale=q_scale, cache=cache,
                        cache_shape=cache_shape if l % 2 == 0 else None,
                        cast=None if last else (w_in, l + 1))
        proj_s, w_out_bf = _inproj(xs, mod_s, norm_pre, w_in_bf, l, q_scale=q_scale,
                                   cast=(w_out, l))
        if not last:
            *res_p, w_in_bf = res_p
        tail_p = dict(x=xp, mod=mod_p, norm_post=norm_post, w_out_bf=w_out_bf, layer=l)
        tail_s = dict(x=xs, mod=mod_s, norm_post=norm_post, w_out_bf=w_out_bf, layer=l)
        if l % 2 == 0:
            proj_p, *cache = res_p
            xp = _ctx_even(proj_p, attn_sink[j], dftc, dftt_p, seq, **tail_p)
            xs = _lat_even(proj_s, attn_sink[j], kc[:, j], vc[:, j], cos, sin, dftc, dftt_s,
                           dec_seq, **tail_s)
        else:
            proj_p, = res_p
            odd = (pool_w_bf, pool_scale, sgu_norm, sgu_w_bf, sgu_b_t, j)
            xp = _odd_mixer(proj_p, *odd, seq, 1, **tail_p)
            xs = _odd_mixer(proj_s, *odd, dec_seq, dec_batch, **tail_s)

    new_k, new_v = (t.reshape(batch, n_attn, seq, N_KV_HEADS, HEAD_DIM) for t in cache)
    return (xp.reshape(batch, seq, D_MODEL), xs.reshape(dec_batch, dec_seq, D_MODEL), new_k, new_v)
```

```python
import functools

import numpy as np
import jax
import jax.numpy as jnp
from jax import lax
from jax.experimental import pallas as pl
from jax.experimental.pallas import tpu as pltpu

F32 = jnp.float32
BF16 = jnp.bfloat16

D_MODEL = 2048
DEPTH = 4
GRID_W = 64
HEAD_DIM = 128
N_Q_HEADS = 12
N_KV_HEADS = 4
Q_PER_KV = N_Q_HEADS // N_KV_HEADS
ATTN_WIDTH = N_Q_HEADS * HEAD_DIM
KV_WIDTH = N_KV_HEADS * HEAD_DIM
WINDOW = 128
BLOCK = 128
ROPE_THETA = 10000.0
FOURIER_GROUPS = 4
FOURIER_WIDTH = D_MODEL // 4
FOURIER_GROUP_DIM = FOURIER_WIDTH // FOURIER_GROUPS
POOL_WINDOWS = (2, 4, 8, 16)
POOL_WIDTH = D_MODEL // 2
POOL_GROUP_DIM = POOL_WIDTH // len(POOL_WINDOWS)
SGU_HEADS = 4
SGU_WIDTH = D_MODEL // 2
SGU_GROUP_DIM = SGU_WIDTH // SGU_HEADS
SGU_CHUNK = 128
IN_WIDTH = 2 * ATTN_WIDTH + 2 * KV_WIDTH + 2 * FOURIER_WIDTH
EPS = 1e-6
NEG_INF = -1e30
ATTN_SCALE = HEAD_DIM ** -0.5

Q_OFF, K_OFF, V_OFF = 0, ATTN_WIDTH, ATTN_WIDTH + KV_WIDTH
GA_OFF = ATTN_WIDTH + 2 * KV_WIDTH
UB_OFF = GA_OFF + ATTN_WIDTH
GB_OFF = UB_OFF + FOURIER_WIDTH
UC_OFF, GC_OFF, UD_OFF, VD_OFF, GD_OFF = (i * POOL_WIDTH for i in range(5))

V7X_VMEM_BYTES = 64 * 1024 * 1024
VMEM_LIMIT = V7X_VMEM_BYTES - 8 * 1024 * 1024

TOKEN_TILE = 512
PROJ_CHUNK = 512
SEQ_TILE = 256
ADA_TILE = 1024
OUT_CHUNK = 256
CTX_SEQS_PER_STEP = 2
COND_ROWS = 8


def _silu(x):
    return x / (1.0 + jnp.exp(-x))


def _params(*sem):
    return pltpu.CompilerParams(dimension_semantics=sem, vmem_limit_bytes=VMEM_LIMIT)


def _dot(a, b):
    return jnp.dot(a, b, preferred_element_type=F32)


def _dot_nt(a, b):
    return lax.dot_general(a, b, (((1,), (1,)), ((), ())), preferred_element_type=F32)


def _ada_kernel(c_ref, w_ref, b_ref, o_ref):
    s = _silu(c_ref[...]).astype(BF16)
    o_ref[...] = _dot(s, w_ref[...].astype(BF16)) + b_ref[...]


def _ada(conds, w_ada, b_ada):
    n = w_ada.shape[-1]
    return pl.pallas_call(
        _ada_kernel,
        grid=(DEPTH, n // ADA_TILE),
        in_specs=[
            pl.BlockSpec((COND_ROWS, D_MODEL), lambda l, j: (0, 0)),
            pl.BlockSpec((None, D_MODEL, ADA_TILE), lambda l, j: (l, 0, j)),
            pl.BlockSpec((None, 1, ADA_TILE), lambda l, j: (l, 0, j)),
        ],
        out_specs=pl.BlockSpec((None, COND_ROWS, ADA_TILE), lambda l, j: (l, 0, j)),
        out_shape=jax.ShapeDtypeStruct((DEPTH, COND_ROWS, n), F32),
        compiler_params=_params("arbitrary", "arbitrary"),
        name="ada_modulation",
    )(conds, w_ada, b_ada.reshape(DEPTH, 1, n))


def _inproj_kernel(x_ref, mod_ref, g_ref, w_ref, *rest, q_scale, n_cache, cache_slot, n_cast):
    outs = rest[n_cache + n_cast:]
    if n_cast:
        outs[-1][...] = rest[n_cache][...].astype(BF16)
        outs = outs[:-1]
    proj_ref, kv_refs = outs[0], outs[1:]
    x = x_ref[...]
    shift = mod_ref[:, 0:D_MODEL]
    scale = mod_ref[:, D_MODEL:2 * D_MODEL]
    ms = jnp.mean(x * x, axis=-1, keepdims=True)
    h = (x * lax.rsqrt(ms + EPS)) * g_ref[...]
    h = (h * (1.0 + scale) + shift).astype(BF16)
    for c0 in range(0, IN_WIDTH, PROJ_CHUNK):
        acc = _dot(h, w_ref[:, c0:c0 + PROJ_CHUNK])
        if kv_refs and c0 in (K_OFF, V_OFF):
            kv_ref = kv_refs[(c0 - K_OFF) // KV_WIDTH]
            seq = kv_ref.shape[-2] // N_KV_HEADS
            for s in range(kv_ref.shape[0]):
                for slot in (range(kv_ref.shape[1]) if kv_ref.ndim == 4 else (None,)):
                    dst = kv_ref.at[s] if slot is None else kv_ref.at[s, slot]
                    if slot not in (None, cache_slot):
                        dst[...] = jnp.zeros(dst.shape, F32)
                        continue
                    for head in range(N_KV_HEADS):
                        dst[pl.ds(head, seq, stride=N_KV_HEADS), :] = (
                            acc[s * seq:(s + 1) * seq, head * HEAD_DIM:(head + 1) * HEAD_DIM])
        if q_scale is not None and c0 < ATTN_WIDTH:
            acc = acc * q_scale
        proj_ref[:, c0:c0 + PROJ_CHUNK] = acc.astype(BF16)


def _inproj(x, mod, norm_pre, w_in_bf, layer, q_scale=None, cache=None, cache_shape=None,
            cast=None):
    assert KV_WIDTH == PROJ_CHUNK and K_OFF % PROJ_CHUNK == 0 and ATTN_WIDTH % PROJ_CHUNK == 0
    m = x.shape[0]
    n_seq = mod.shape[1]
    n_steps = m // TOKEN_TILE
    tiles_per_seq = n_steps // n_seq
    in_specs = [
        pl.BlockSpec((TOKEN_TILE, D_MODEL), lambda i: (i, 0)),
        pl.BlockSpec((None, None, 1, 3 * D_MODEL), lambda i: (layer, i // tiles_per_seq, 0, 0)),
        pl.BlockSpec((None, 1, D_MODEL), lambda i: (layer, 0, 0)),
        pl.BlockSpec(w_in_bf.shape, lambda i: (0, 0), pipeline_mode=pl.Buffered(1)),
    ]
    args = [x, mod, norm_pre.reshape(DEPTH, 1, D_MODEL), w_in_bf]
    out_shape = [jax.ShapeDtypeStruct((m, IN_WIDTH), BF16)]
    out_specs = [pl.BlockSpec((TOKEN_TILE, IN_WIDTH), lambda i: (i, 0))]
    aliases = {}
    n_cache = 0
    if cache_shape is not None:
        n_slots, seq_rows = cache_shape[1:3]
        seqs_per_tile = TOKEN_TILE * N_KV_HEADS // seq_rows
        slot = layer // 2
        if cache is None:
            kv_block = pl.BlockSpec((seqs_per_tile, n_slots, seq_rows, HEAD_DIM),
                                    lambda i: (i, 0, 0, 0))
        else:
            kv_block = pl.BlockSpec((seqs_per_tile, None, seq_rows, HEAD_DIM),
                                    lambda i: (i, slot, 0, 0))
        out_shape += [jax.ShapeDtypeStruct(cache_shape, F32)] * 2
        out_specs += [kv_block] * 2
        if cache is not None:
            n_cache = len(cache)
            aliases = {len(args) + t: 1 + t for t in range(n_cache)}
            in_specs += [pl.BlockSpec(memory_space=pl.ANY)] * n_cache
            args += list(cache)
    if cast is not None:
        weights, index = cast
        rows, cols = weights.shape[1:]
        in_specs.append(pl.BlockSpec((None, rows // n_steps, cols), lambda i: (index, i, 0)))
        args.append(weights)
        out_shape.append(jax.ShapeDtypeStruct((rows, cols), BF16))
        out_specs.append(pl.BlockSpec((rows // n_steps, cols), lambda i: (i, 0)))
    return pl.pallas_call(
        functools.partial(_inproj_kernel, q_scale=q_scale, n_cache=n_cache,
                          cache_slot=layer // 2, n_cast=int(cast is not None)),
        grid=(n_steps,),
        in_specs=in_specs,
        out_specs=out_specs,
        out_shape=out_shape,
        input_output_aliases=aliases,
        compiler_params=_params("arbitrary"),
        name="in_projection",
    )(*args)


def _mixer_call(mixer, n_tiles, in_specs, args, scratch, x, mod, norm_post, w_out_bf, layer,
                rows, mod_row, name):
    n_in = len(in_specs)

    def mixer_tile(t):
        return jnp.minimum(t, n_tiles - 1)

    def tail_tile(t):
        return jnp.maximum(t - 2, 0)

    def clamp(spec):
        if spec.index_map is None:
            return spec
        return pl.BlockSpec(spec.block_shape, lambda t: spec.index_map(mixer_tile(t)),
                            pipeline_mode=spec.pipeline_mode)

    tail_specs = [
        pl.BlockSpec(w_out_bf.shape, lambda t: (0, 0), pipeline_mode=pl.Buffered(1)),
        pl.BlockSpec((rows, D_MODEL), lambda t: (tail_tile(t), 0)),
        pl.BlockSpec((None, None, 1, 3 * D_MODEL),
                     lambda t: (layer, mod_row(tail_tile(t)), 0, 0)),
        pl.BlockSpec((None, 1, D_MODEL), lambda t: (layer, 0, 0)),
    ]

    def body(*refs):
        mixer_refs = refs[:n_in]
        w_ref, x_ref, mod_ref, g_ref, out_ref = refs[n_in:n_in + 5]
        o_even, o_odd, acc_even, acc_odd = refs[n_in + 5:n_in + 9]
        extra = refs[n_in + 9:]
        t = pl.program_id(0)

        @pl.when(t == 0)
        def _():
            o_odd[...] = jnp.zeros_like(o_odd)
            acc_odd[...] = jnp.zeros_like(acc_odd)

        def step(o_read, o_write, acc_read, acc_write, with_mixer):
            acc = acc_read[...]
            ms = jnp.mean(acc * acc, axis=-1, keepdims=True)
            y = (acc * lax.rsqrt(ms + EPS)) * g_ref[...]
            out_ref[...] = x_ref[...] + mod_ref[:, 2 * D_MODEL:3 * D_MODEL] * y

            def project(c0):
                acc_write[:, c0:c0 + OUT_CHUNK] = _dot(o_read[...], w_ref[:, c0:c0 + OUT_CHUNK])

            chunks = iter(range(0, D_MODEL, OUT_CHUNK))
            if with_mixer:
                for _ in mixer(*mixer_refs, o_write, *extra, tile=mixer_tile(t)):
                    c0 = next(chunks, None)
                    if c0 is not None:
                        project(c0)
            for c0 in chunks:
                project(c0)

        even = t % 2 == 0
        live = t < n_tiles
        pl.when(live & even)(lambda: step(o_odd, o_even, acc_odd, acc_even, True))
        pl.when(live & ~even)(lambda: step(o_even, o_odd, acc_even, acc_odd, True))
        pl.when(~live & even)(lambda: step(o_odd, o_even, acc_odd, acc_even, False))
        pl.when(~live & ~even)(lambda: step(o_even, o_odd, acc_even, acc_odd, False))

    return pl.pallas_call(
        body,
        grid=(n_tiles + 2,),
        in_specs=[clamp(s) for s in in_specs] + tail_specs,
        out_specs=pl.BlockSpec((rows, D_MODEL), lambda t: (tail_tile(t), 0)),
        out_shape=jax.ShapeDtypeStruct(x.shape, F32),
        scratch_shapes=([pltpu.VMEM((rows, D_MODEL), BF16)] * 2
                        + [pltpu.VMEM((rows, D_MODEL), F32)] * 2 + list(scratch)),
        compiler_params=_params("arbitrary"),
        name=name,
    )(*args, w_out_bf, x, mod, norm_post.reshape(DEPTH, 1, D_MODEL))


def _softmax_pv(scores, sink, values):
    m = sink
    for s in scores:
        m = jnp.maximum(m, jnp.max(s, axis=-1, keepdims=True))
    denom = jnp.exp(sink - m)
    out = None
    for s, v in zip(scores, values):
        e = jnp.exp(s - m)
        denom = denom + jnp.sum(e, axis=-1, keepdims=True)
        pv = _dot(e.astype(BF16), v)
        out = pv if out is None else out + pv
    return out / denom


def _fourier_cols(u_ref, col0, dftc):
    tops, bots = [], []
    for g in range(FOURIER_GROUPS):
        c = col0 + g * FOURIER_GROUP_DIM
        y = _dot(u_ref[:, c:c + FOURIER_GROUP_DIM], dftc)
        tops.append(y[:, :FOURIER_GROUP_DIM].astype(BF16))
        bots.append(y[:, FOURIER_GROUP_DIM:].astype(BF16))
    return jnp.concatenate(tops, axis=1), jnp.concatenate(bots, axis=1)


def _rope(x, cos, sin_signed):
    lane = lax.broadcasted_iota(jnp.int32, x.shape, 1)
    partner = jnp.where((lane % 64) < 32,
                        pltpu.roll(x, HEAD_DIM - 32, 1), pltpu.roll(x, 32, 1))
    return x * cos + partner * sin_signed


def _per_sequence(mixer, row_args, seq, n_seq):
    def run(*refs, tile):
        for i in range(n_seq):
            views = [r.at[i * seq:(i + 1) * seq] if k in row_args else r
                     for k, r in enumerate(refs)]
            yield from mixer(*views, tile=tile)
    return run


def _ctx_even_mixer(sink_ref, p_ref, dftc_ref, dftt_ref, o_ref, *, tile):
    del tile
    scores = []
    for hq in range(N_Q_HEADS):
        h = hq // Q_PER_KV
        k = p_ref[:, K_OFF + h * HEAD_DIM:K_OFF + (h + 1) * HEAD_DIM]
        scores.append(_dot_nt(p_ref[:, hq * HEAD_DIM:(hq + 1) * HEAD_DIM], k))
    top, bot = _fourier_cols(p_ref, UB_OFF, dftc_ref[...])
    f = _dot(dftt_ref[...], jnp.concatenate([top, bot], axis=0))
    yield
    for hq in range(N_Q_HEADS):
        h = hq // Q_PER_KV
        v = p_ref[:, V_OFF + h * HEAD_DIM:V_OFF + (h + 1) * HEAD_DIM]
        o = _softmax_pv([scores[hq]], sink_ref[hq], [v])
        ga = p_ref[:, GA_OFF + hq * HEAD_DIM:GA_OFF + (hq + 1) * HEAD_DIM].astype(F32)
        o_ref[:, hq * HEAD_DIM:(hq + 1) * HEAD_DIM] = (_silu(ga) * o).astype(BF16)
        yield
    gb = p_ref[:, GB_OFF:GB_OFF + FOURIER_WIDTH].astype(F32)
    o_ref[:, ATTN_WIDTH:] = (_silu(gb) * f).astype(BF16)
    yield


def _ctx_even(proj, sink, dftc, dftt, seq, **tail):
    m = proj.shape[0]
    rows = CTX_SEQS_PER_STEP * seq
    return _mixer_call(
        _per_sequence(_ctx_even_mixer, (1, 4), seq, CTX_SEQS_PER_STEP), m // rows,
        [pl.BlockSpec(memory_space=pltpu.SMEM),
         pl.BlockSpec((rows, IN_WIDTH), lambda b: (b, 0)),
         pl.BlockSpec(dftc.shape, lambda b: (0, 0)),
         pl.BlockSpec(dftt.shape, lambda b: (0, 0))],
        [sink, proj, dftc, dftt], [],
        rows=rows, mod_row=lambda b: 0, name="context_attention_fourier", **tail)


def _lat_even_mixer(sink_ref, row_ref, k_ref, v_ref, kc_ref, vc_ref, cos_ref, sin_ref,
                    u_ref, dftc_ref, dftt_ref, o_ref, kr_ref, z_ref, *, tile, seq):
    i = tile % (seq // BLOCK)
    rope_rows = 256

    @pl.when(i == 0)
    def _():
        def body(c, carry):
            r = pl.multiple_of(c * rope_rows, rope_rows)
            cos = cos_ref[pl.ds(r, rope_rows), :]
            sin = sin_ref[pl.ds(r, rope_rows), :]
            for h in range(N_KV_HEADS):
                cols = slice(h * HEAD_DIM, (h + 1) * HEAD_DIM)
                kr_ref[pl.ds(r, rope_rows), cols] = _rope(
                    k_ref[pl.ds(r, rope_rows), cols].astype(F32), cos, sin).astype(BF16)
            return carry
        lax.fori_loop(0, seq // rope_rows, body, 0)
        top, bot = _fourier_cols(u_ref, 0, dftc_ref[...])
        z_ref[0:seq, :] = top
        z_ref[seq:2 * seq, :] = bot

    span = 3 * BLOCK
    q0 = pl.multiple_of(i * BLOCK, BLOCK)
    start = pl.multiple_of(jnp.clip(i * BLOCK - BLOCK, 0, seq - span), BLOCK)
    cos = cos_ref[pl.ds(q0, BLOCK), :]
    sin = sin_ref[pl.ds(q0, BLOCK), :]
    dist = ((q0 - start) + lax.broadcasted_iota(jnp.int32, (BLOCK, span), 0)
            - lax.broadcasted_iota(jnp.int32, (BLOCK, span), 1))
    valid = (dist >= -WINDOW) & (dist <= WINDOW)
    valid = jnp.concatenate([valid] * Q_PER_KV, axis=0)
    scores = []
    for h in range(N_KV_HEADS):
        kv_cols = slice(h * HEAD_DIM, (h + 1) * HEAD_DIM)
        q = jnp.concatenate(
            [_rope(row_ref[:, (h * Q_PER_KV + g) * HEAD_DIM:(h * Q_PER_KV + g + 1) * HEAD_DIM]
                   .astype(F32), cos, sin).astype(BF16) for g in range(Q_PER_KV)], axis=0)
        s_loc = _dot_nt(q, kr_ref[pl.ds(start, span), kv_cols])
        scores.append((jnp.where(valid, s_loc, NEG_INF), _dot_nt(q, kc_ref[:, kv_cols])))
    f = _dot(dftt_ref[...], z_ref[...])
    yield
    for h in range(N_KV_HEADS):
        kv_cols = slice(h * HEAD_DIM, (h + 1) * HEAD_DIM)
        sink = jnp.concatenate(
            [jnp.full((BLOCK, 1), sink_ref[h * Q_PER_KV + g], F32) for g in range(Q_PER_KV)],
            axis=0)
        o = _softmax_pv(list(scores[h]), sink,
                        [v_ref[pl.ds(start, span), kv_cols], vc_ref[:, kv_cols]])
        yield
        for g in range(Q_PER_KV):
            hq = h * Q_PER_KV + g
            ga = row_ref[:, GA_OFF + hq * HEAD_DIM:GA_OFF + (hq + 1) * HEAD_DIM].astype(F32)
            o_ref[:, hq * HEAD_DIM:(hq + 1) * HEAD_DIM] = (
                _silu(ga) * o[g * BLOCK:(g + 1) * BLOCK]).astype(BF16)
        yield
    gb = row_ref[:, GB_OFF:GB_OFF + FOURIER_WIDTH].astype(F32)
    o_ref[:, ATTN_WIDTH:] = (_silu(gb) * f).astype(BF16)
    yield


def _lat_even(proj, sink, kc, vc, cos, sin, dftc, dftt, seq, **tail):
    m = proj.shape[0]
    nb = seq // BLOCK
    return _mixer_call(
        functools.partial(_lat_even_mixer, seq=seq), m // BLOCK,
        [pl.BlockSpec(memory_space=pltpu.SMEM),
         pl.BlockSpec((BLOCK, IN_WIDTH), lambda t: (t, 0)),
         pl.BlockSpec((seq, KV_WIDTH), lambda t: (t // nb, K_OFF // KV_WIDTH)),
         pl.BlockSpec((seq, KV_WIDTH), lambda t: (t // nb, V_OFF // KV_WIDTH)),
         pl.BlockSpec((None,) + kc.shape[1:], lambda t: (t // nb, 0, 0)),
         pl.BlockSpec((None,) + vc.shape[1:], lambda t: (t // nb, 0, 0)),
         pl.BlockSpec(cos.shape, lambda t: (0, 0)),
         pl.BlockSpec(sin.shape, lambda t: (0, 0)),
         pl.BlockSpec((seq, FOURIER_WIDTH), lambda t: (t // nb, UB_OFF // FOURIER_WIDTH)),
         pl.BlockSpec(dftc.shape, lambda t: (0, 0)),
         pl.BlockSpec((BLOCK, 2 * seq), lambda t: (t % nb, 0))],
        [sink, proj, proj, proj, kc, vc, cos, sin, proj, dftc, dftt],
        [pltpu.VMEM((seq, KV_WIDTH), BF16), pltpu.VMEM((2 * seq, FOURIER_WIDTH), BF16)],
        rows=BLOCK, mod_row=lambda t: t // nb, name="latent_attention_fourier", **tail)


def _odd_mixer_body(uc_ref, gc_ref, ud_ref, vd_ref, gd_ref, pw_ref, ps_ref, sn_ref, sw_ref,
                    sb_ref, o_ref, *, tile, seq):
    r = tile % (seq // SEQ_TILE)
    span = min(2 * SEQ_TILE, seq)
    t0 = pl.multiple_of(r * SEQ_TILE, SEQ_TILE)
    start = pl.multiple_of(jnp.clip(r * SEQ_TILE - SEQ_TILE // 2, 0, seq - span), SEQ_TILE // 2)
    rel = (lax.broadcasted_iota(jnp.int32, (SEQ_TILE, span), 1) + (start - t0)
           - lax.broadcasted_iota(jnp.int32, (SEQ_TILE, span), 0))
    pos = t0 + lax.broadcasted_iota(jnp.int32, (SEQ_TILE, 1), 0)
    for g, w in enumerate(POOL_WINDOWS):
        cols = slice(g * POOL_GROUP_DIM, (g + 1) * POOL_GROUP_DIM)
        band = jnp.where((rel >= -(w // 2)) & (rel < w // 2), 1.0, 0.0).astype(BF16)
        total = _dot(band, uc_ref[pl.ds(start, span), cols])
        cnt = (jnp.minimum(pos + w // 2, seq) - jnp.maximum(pos - w // 2, 0)).astype(F32)
        pooled = total / cnt - uc_ref[pl.ds(t0, SEQ_TILE), cols].astype(F32)
        mixed = _dot(pooled.astype(BF16), pw_ref[g]) * ps_ref[:, cols]
        o_ref[:, cols] = (_silu(gc_ref[:, cols].astype(F32)) * mixed).astype(BF16)
        yield
    for h in range(SGU_HEADS):
        cols = slice(h * SGU_GROUP_DIM, (h + 1) * SGU_GROUP_DIM)
        v = vd_ref[:, cols].astype(F32)
        mu = jnp.mean(v, axis=-1, keepdims=True)
        vc = v - mu
        var = jnp.mean(vc * vc, axis=-1, keepdims=True)
        vn = (vc * lax.rsqrt(var + EPS) * sn_ref[:, cols]).astype(BF16)
        bias = sb_ref[:, h:h + 1]
        for c in range(SEQ_TILE // SGU_CHUNK):
            rows = slice(c * SGU_CHUNK, (c + 1) * SGU_CHUNK)
            gated = ud_ref[rows, cols].astype(F32) * (_dot(sw_ref[h], vn[rows]) + bias)
            o_ref[rows, POOL_WIDTH + h * SGU_GROUP_DIM:POOL_WIDTH + (h + 1) * SGU_GROUP_DIM] = (
                _silu(gd_ref[rows, cols].astype(F32)) * gated).astype(BF16)
        yield


def _odd_mixer(proj, pool_w_bf, pool_scale, sgu_norm, sgu_w_bf, sgu_b_t, j, seq, n_mod, **tail):
    m = proj.shape[0]
    body = functools.partial(_odd_mixer_body, seq=seq)
    rows = SEQ_TILE
    if seq == SEQ_TILE:
        rows = CTX_SEQS_PER_STEP * seq
        body = _per_sequence(body, (0, 1, 2, 3, 4, 10), seq, CTX_SEQS_PER_STEP)
    uc_rows = max(rows, seq)
    steps_per_seq = uc_rows // rows

    def tile_spec(off):
        return pl.BlockSpec((rows, POOL_WIDTH), lambda t: (t, off // POOL_WIDTH))

    return _mixer_call(
        body, m // rows,
        [pl.BlockSpec((uc_rows, POOL_WIDTH),
                      lambda t: (t // steps_per_seq, UC_OFF // POOL_WIDTH)),
         tile_spec(GC_OFF), tile_spec(UD_OFF), tile_spec(VD_OFF), tile_spec(GD_OFF),
         pl.BlockSpec((None,) + pool_w_bf.shape[1:], lambda t: (j, 0, 0, 0)),
         pl.BlockSpec((None, 1, POOL_WIDTH), lambda t: (j, 0, 0)),
         pl.BlockSpec((None, 1, SGU_WIDTH), lambda t: (j, 0, 0)),
         pl.BlockSpec((None,) + sgu_w_bf.shape[1:], lambda t: (j, 0, 0, 0)),
         pl.BlockSpec((None,) + sgu_b_t.shape[1:], lambda t: (j, 0, 0))],
        [proj, proj, proj, proj, proj, pool_w_bf, pool_scale.reshape(-1, 1, POOL_WIDTH),
         sgu_norm.reshape(-1, 1, SGU_WIDTH), sgu_w_bf, sgu_b_t], [],
        rows=rows, mod_row=(lambda t: t // steps_per_seq) if n_mod > 1 else (lambda t: 0),
        name="pool_spatial_gating", **tail)


def _dft_tables(seq):
    def cos_sin(n):
        idx = np.arange(n)
        ang = 2.0 * np.pi * ((idx[:, None] * idx[None, :]) % n) / n
        return np.cos(ang) / np.sqrt(n), np.sin(ang) / np.sqrt(n)

    cc, sc = cos_sin(FOURIER_GROUP_DIM)
    ct, st = cos_sin(seq)
    dftc = jnp.asarray(np.concatenate([cc, sc], axis=1), dtype=F32).astype(BF16)
    dftt = jnp.asarray(np.concatenate([ct, -st], axis=1), dtype=F32).astype(BF16)
    return dftc, dftt


def _rope_tables(seq):
    n_freq = HEAD_DIM // 4
    t = np.arange(seq)
    inv = ROPE_THETA ** (-np.arange(n_freq, dtype=np.float64) / n_freq)
    ang = np.stack([(t // GRID_W)[:, None] * inv, (t % GRID_W)[:, None] * inv], axis=1)
    cos = np.repeat(np.cos(ang)[:, :, None, :], 2, axis=2).reshape(seq, HEAD_DIM)
    sin = np.sin(ang)
    sin = np.stack([-sin, sin], axis=2).reshape(seq, HEAD_DIM)
    return jnp.asarray(cos, dtype=F32), jnp.asarray(sin, dtype=F32)


def kernel(x_prompt, x_sample, cache_k, cache_v, c, c_ctx, w_ada, b_ada, norm_pre, norm_post,
           w_in, w_out, attn_sink, pool_w, pool_scale, sgu_norm, sgu_w, sgu_b):
    batch, seq, _ = x_prompt.shape
    dec_batch, dec_seq, _ = x_sample.shape
    n_attn, past = cache_k.shape[1], cache_k.shape[2]

    pool_w_bf = pool_w.astype(BF16)
    sgu_w_bf = sgu_w.astype(BF16)
    sgu_b_t = jnp.swapaxes(sgu_b, 1, 2)
    kc = cache_k.reshape(dec_batch, n_attn, past, KV_WIDTH).astype(BF16)
    vc = cache_v.reshape(dec_batch, n_attn, past, KV_WIDTH).astype(BF16)

    conds = jnp.concatenate(
        [c_ctx[None, :], c, jnp.zeros((COND_ROWS - 1 - dec_batch, D_MODEL), F32)], axis=0)
    mod = _ada(conds, w_ada, b_ada)
    mod_p = mod[:, 0:1].reshape(DEPTH, 1, 1, 3 * D_MODEL)
    mod_s = mod[:, 1:1 + dec_batch].reshape(DEPTH, dec_batch, 1, 3 * D_MODEL)

    dftc, dftt_p = _dft_tables(seq)
    _, dftt_s = _dft_tables(dec_seq)
    cos, sin = _rope_tables(dec_seq)

    xp = x_prompt.reshape(batch * seq, D_MODEL)
    xs = x_sample.reshape(dec_batch * dec_seq, D_MODEL)
    cache = None
    cache_shape = (batch, n_attn, seq * N_KV_HEADS, HEAD_DIM)
    w_in_bf = w_in[0].astype(BF16)
    for l in range(DEPTH):
        j = l // 2
        q_scale = ATTN_SCALE if l % 2 == 0 else None
        last = l == DEPTH - 1
        res_p = _inproj(xp, mod_p, norm_pre, w_in_bf, l, q_scale=q_scale, cache=cache,
                        cache_shape=cache_shape if l % 2 == 0 else None,
                        cast=None if last else (w_in, l + 1))
        proj_s, w_out_bf = _inproj(xs, mod_s, norm_pre, w_in_bf, l, q_scale=q_scale,
                                   cast=(w_out, l))
        if not last:
            *res_p, w_in_bf = res_p
        tail_p = dict(x=xp, mod=mod_p, norm_post=norm_post, w_out_bf=w_out_bf, layer=l)
        tail_s = dict(x=xs, mod=mod_s, norm_post=norm_post, w_out_bf=w_out_bf, layer=l)
        if l % 2 == 0:
            proj_p, *cache = res_p
            xp = _ctx_even(proj_p, attn_sink[j], dftc, dftt_p, seq, **tail_p)
            xs = _lat_even(proj_s, attn_sink[j], kc[:, j], vc[:, j], cos, sin, dftc, dftt_s,
                           dec_seq, **tail_s)
        else:
            proj_p, = res_p
            odd = (pool_w_bf, pool_scale, sgu_norm, sgu_w_bf, sgu_b_t, j)
            xp = _odd_mixer(proj_p, *odd, seq, 1, **tail_p)
            xs = _odd_mixer(proj_s, *odd, dec_seq, dec_batch, **tail_s)

    new_k, new_v = (t.reshape(batch, n_attn, seq, N_KV_HEADS, HEAD_DIM) for t in cache)
    return (xp.reshape(batch, seq, D_MODEL), xs.reshape(dec_batch, dec_seq, D_MODEL), new_k, new_v)
```

```python
import functools

import numpy as np
import jax
import jax.numpy as jnp
from jax import lax
from jax.experimental import pallas as pl
from jax.experimental.pallas import tpu as pltpu

F32 = jnp.float32
BF16 = jnp.bfloat16

D_MODEL = 2048
DEPTH = 4
GRID_W = 64
HEAD_DIM = 128
N_Q_HEADS = 12
N_KV_HEADS = 4
Q_PER_KV = N_Q_HEADS // N_KV_HEADS
ATTN_WIDTH = N_Q_HEADS * HEAD_DIM
KV_WIDTH = N_KV_HEADS * HEAD_DIM
WINDOW = 128
BLOCK = 128
ROPE_THETA = 10000.0
FOURIER_GROUPS = 4
FOURIER_WIDTH = D_MODEL // 4
FOURIER_GROUP_DIM = FOURIER_WIDTH // FOURIER_GROUPS
POOL_WINDOWS = (2, 4, 8, 16)
POOL_WIDTH = D_MODEL // 2
POOL_GROUP_DIM = POOL_WIDTH // len(POOL_WINDOWS)
SGU_HEADS = 4
SGU_WIDTH = D_MODEL // 2
SGU_GROUP_DIM = SGU_WIDTH // SGU_HEADS
SGU_CHUNK = 128
IN_WIDTH = 2 * ATTN_WIDTH + 2 * KV_WIDTH + 2 * FOURIER_WIDTH
EPS = 1e-6
NEG_INF = -1e30
ATTN_SCALE = HEAD_DIM ** -0.5

Q_OFF, K_OFF, V_OFF = 0, ATTN_WIDTH, ATTN_WIDTH + KV_WIDTH
GA_OFF = ATTN_WIDTH + 2 * KV_WIDTH
UB_OFF = GA_OFF + ATTN_WIDTH
GB_OFF = UB_OFF + FOURIER_WIDTH
UC_OFF, GC_OFF, UD_OFF, VD_OFF, GD_OFF = (i * POOL_WIDTH for i in range(5))

V7X_VMEM_BYTES = 64 * 1024 * 1024
VMEM_LIMIT = V7X_VMEM_BYTES - 8 * 1024 * 1024

TOKEN_TILE = 512
PROJ_CHUNK = 512
SEQ_TILE = 256
ADA_TILE = 1024
OUT_CHUNK = 256
CTX_SEQS_PER_STEP = 2
COND_ROWS = 8


def _silu(x):
    return x / (1.0 + jnp.exp(-x))


def _params(*sem):
    return pltpu.CompilerParams(dimension_semantics=sem, vmem_limit_bytes=VMEM_LIMIT)


def _dot(a, b):
    return jnp.dot(a, b, preferred_element_type=F32)


def _dot_nt(a, b):
    return lax.dot_general(a, b, (((1,), (1,)), ((), ())), preferred_element_type=F32)


def _ada_kernel(c_ref, w_ref, b_ref, o_ref):
    s = _silu(c_ref[...]).astype(BF16)
    o_ref[...] = _dot(s, w_ref[...].astype(BF16)) + b_ref[...]


def _ada(conds, w_ada, b_ada):
    n = w_ada.shape[-1]
    return pl.pallas_call(
        _ada_kernel,
        grid=(DEPTH, n // ADA_TILE),
        in_specs=[
            pl.BlockSpec((COND_ROWS, D_MODEL), lambda l, j: (0, 0)),
            pl.BlockSpec((None, D_MODEL, ADA_TILE), lambda l, j: (l, 0, j)),
            pl.BlockSpec((None, 1, ADA_TILE), lambda l, j: (l, 0, j)),
        ],
        out_specs=pl.BlockSpec((None, COND_ROWS, ADA_TILE), lambda l, j: (l, 0, j)),
        out_shape=jax.ShapeDtypeStruct((DEPTH, COND_ROWS, n), F32),
        compiler_params=_params("arbitrary", "arbitrary"),
        name="ada_modulation",
    )(conds, w_ada, b_ada.reshape(DEPTH, 1, n))


def _inproj_kernel(x_ref, mod_ref, g_ref, w_ref, *rest, q_scale, n_cache, cache_slot, n_cast):
    outs = rest[n_cache + n_cast:]
    if n_cast:
        outs[-1][...] = rest[n_cache][...].astype(BF16)
        outs = outs[:-1]
    proj_ref, kv_refs = outs[0], outs[1:]
    x = x_ref[...]
    shift = mod_ref[:, 0:D_MODEL]
    scale = mod_ref[:, D_MODEL:2 * D_MODEL]
    ms = jnp.mean(x * x, axis=-1, keepdims=True)
    h = (x * lax.rsqrt(ms + EPS)) * g_ref[...]
    h = (h * (1.0 + scale) + shift).astype(BF16)
    for c0 in range(0, IN_WIDTH, PROJ_CHUNK):
        acc = _dot(h, w_ref[:, c0:c0 + PROJ_CHUNK])
        if kv_refs and c0 in (K_OFF, V_OFF):
            kv_ref = kv_refs[(c0 - K_OFF) // KV_WIDTH]
            seq = kv_ref.shape[-2] // N_KV_HEADS
            for s in range(kv_ref.shape[0]):
                for slot in (range(kv_ref.shape[1]) if kv_ref.ndim == 4 else (None,)):
                    dst = kv_ref.at[s] if slot is None else kv_ref.at[s, slot]
                    if slot not in (None, cache_slot):
                        dst[...] = jnp.zeros(dst.shape, F32)
                        continue
                    for head in range(N_KV_HEADS):
                        dst[pl.ds(head, seq, stride=N_KV_HEADS), :] = (
                            acc[s * seq:(s + 1) * seq, head * HEAD_DIM:(head + 1) * HEAD_DIM])
        if q_scale is not None and c0 < ATTN_WIDTH:
            acc = acc * q_scale
        proj_ref[:, c0:c0 + PROJ_CHUNK] = acc.astype(BF16)


def _inproj(x, mod, norm_pre, w_in_bf, layer, q_scale=None, cache=None, cache_shape=None,
            cast=None):
    assert KV_WIDTH == PROJ_CHUNK and K_OFF % PROJ_CHUNK == 0 and ATTN_WIDTH % PROJ_CHUNK == 0
    m = x.shape[0]
    n_seq = mod.shape[1]
    n_steps = m // TOKEN_TILE
    tiles_per_seq = n_steps // n_seq
    in_specs = [
        pl.BlockSpec((TOKEN_TILE, D_MODEL), lambda i: (i, 0)),
        pl.BlockSpec((None, None, 1, 3 * D_MODEL), lambda i: (layer, i // tiles_per_seq, 0, 0)),
        pl.BlockSpec((None, 1, D_MODEL), lambda i: (layer, 0, 0)),
        pl.BlockSpec(w_in_bf.shape, lambda i: (0, 0), pipeline_mode=pl.Buffered(1)),
    ]
    args = [x, mod, norm_pre.reshape(DEPTH, 1, D_MODEL), w_in_bf]
    out_shape = [jax.ShapeDtypeStruct((m, IN_WIDTH), BF16)]
    out_specs = [pl.BlockSpec((TOKEN_TILE, IN_WIDTH), lambda i: (i, 0))]
    aliases = {}
    n_cache = 0
    if cache_shape is not None:
        n_slots, seq_rows = cache_shape[1:3]
        seqs_per_tile = TOKEN_TILE * N_KV_HEADS // seq_rows
        slot = layer // 2
        if cache is None:
            kv_block = pl.BlockSpec((seqs_per_tile, n_slots, seq_rows, HEAD_DIM),
                                    lambda i: (i, 0, 0, 0))
        else:
            kv_block = pl.BlockSpec((seqs_per_tile, None, seq_rows, HEAD_DIM),
                                    lambda i: (i, slot, 0, 0))
        out_shape += [jax.ShapeDtypeStruct(cache_shape, F32)] * 2
        out_specs += [kv_block] * 2
        if cache is not None:
            n_cache = len(cache)
            aliases = {len(args) + t: 1 + t for t in range(n_cache)}
            in_specs += [pl.BlockSpec(memory_space=pl.ANY)] * n_cache
            args += list(cache)
    if cast is not None:
        weights, index = cast
        rows, cols = weights.shape[1:]
        in_specs.append(pl.BlockSpec((None, rows // n_steps, cols), lambda i: (index, i, 0)))
        args.append(weights)
        out_shape.append(jax.ShapeDtypeStruct((rows, cols), BF16))
        out_specs.append(pl.BlockSpec((rows // n_steps, cols), lambda i: (i, 0)))
    return pl.pallas_call(
        functools.partial(_inproj_kernel, q_scale=q_scale, n_cache=n_cache,
                          cache_slot=layer // 2, n_cast=int(cast is not None)),
        grid=(n_steps,),
        in_specs=in_specs,
        out_specs=out_specs,
        out_shape=out_shape,
        input_output_aliases=aliases,
        compiler_params=_params("arbitrary"),
        name="in_projection",
    )(*args)


def _mixer_call(mixer, n_tiles, in_specs, args, scratch, x, mod, norm_post, w_out_bf, layer,
                rows, mod_row, name):
    n_in = len(in_specs)

    def mixer_tile(t):
        return jnp.minimum(t, n_tiles - 1)

    def tail_tile(t):
        return jnp.maximum(t - 2, 0)

    def clamp(spec):
        if spec.index_map is None:
            return spec
        return pl.BlockSpec(spec.block_shape, lambda t: spec.index_map(mixer_tile(t)),
                            pipeline_mode=spec.pipeline_mode)

    tail_specs = [
        pl.BlockSpec(w_out_bf.shape, lambda t: (0, 0), pipeline_mode=pl.Buffered(1)),
        pl.BlockSpec((rows, D_MODEL), lambda t: (tail_tile(t), 0)),
        pl.BlockSpec((None, None, 1, 3 * D_MODEL),
                     lambda t: (layer, mod_row(tail_tile(t)), 0, 0)),
        pl.BlockSpec((None, 1, D_MODEL), lambda t: (layer, 0, 0)),
    ]

    def body(*refs):
        mixer_refs = refs[:n_in]
        w_ref, x_ref, mod_ref, g_ref, out_ref = refs[n_in:n_in + 5]
        o_even, o_odd, acc_even, acc_odd = refs[n_in + 5:n_in + 9]
        extra = refs[n_in + 9:]
        t = pl.program_id(0)

        @pl.when(t == 0)
        def _():
            o_odd[...] = jnp.zeros_like(o_odd)
            acc_odd[...] = jnp.zeros_like(acc_odd)

        def step(parity, stages):
            o_read, o_write, acc_read, acc_write = (
                (o_odd, o_even, acc_odd, acc_even) if parity == 0
                else (o_even, o_odd, acc_even, acc_odd))
            acc = acc_read[...]
            ms = jnp.mean(acc * acc, axis=-1, keepdims=True)
            y = (acc * lax.rsqrt(ms + EPS)) * g_ref[...]
            out_ref[...] = x_ref[...] + mod_ref[:, 2 * D_MODEL:3 * D_MODEL] * y
            if stages == 1:
                return

            def project(c0):
                acc_write[:, c0:c0 + OUT_CHUNK] = _dot(o_read[...], w_ref[:, c0:c0 + OUT_CHUNK])

            chunks = iter(range(0, D_MODEL, OUT_CHUNK))
            if stages == 3:
                for _ in mixer(*mixer_refs, o_write, *extra, tile=mixer_tile(t)):
                    c0 = next(chunks, None)
                    if c0 is not None:
                        project(c0)
            for c0 in chunks:
                project(c0)

        live = t < n_tiles
        pl.when(live & (t % 2 == 0))(lambda: step(0, 3))
        pl.when(live & (t % 2 == 1))(lambda: step(1, 3))
        pl.when(t == n_tiles)(lambda: step(n_tiles % 2, 2))
        pl.when(t == n_tiles + 1)(lambda: step((n_tiles + 1) % 2, 1))

    return pl.pallas_call(
        body,
        grid=(n_tiles + 2,),
        in_specs=[clamp(s) for s in in_specs] + tail_specs,
        out_specs=pl.BlockSpec((rows, D_MODEL), lambda t: (tail_tile(t), 0)),
        out_shape=jax.ShapeDtypeStruct(x.shape, F32),
        scratch_shapes=([pltpu.VMEM((rows, D_MODEL), BF16)] * 2
                        + [pltpu.VMEM((rows, D_MODEL), F32)] * 2 + list(scratch)),
        compiler_params=_params("arbitrary"),
        name=name,
    )(*args, w_out_bf, x, mod, norm_post.reshape(DEPTH, 1, D_MODEL))


def _softmax_pv(scores, sink, values):
    m = sink
    for s in scores:
        m = jnp.maximum(m, jnp.max(s, axis=-1, keepdims=True))
    denom = jnp.exp(sink - m)
    out = None
    for s, v in zip(scores, values):
        e = jnp.exp(s - m)
        denom = denom + jnp.sum(e, axis=-1, keepdims=True)
        pv = _dot(e.astype(BF16), v)
        out = pv if out is None else out + pv
    return out / denom


def _fourier_cols(u_ref, col0, dftc):
    tops, bots = [], []
    for g in range(FOURIER_GROUPS):
        c = col0 + g * FOURIER_GROUP_DIM
        y = _dot(u_ref[:, c:c + FOURIER_GROUP_DIM], dftc)
        tops.append(y[:, :FOURIER_GROUP_DIM].astype(BF16))
        bots.append(y[:, FOURIER_GROUP_DIM:].astype(BF16))
    return jnp.concatenate(tops, axis=1), jnp.concatenate(bots, axis=1)


def _rope(x, cos, sin_signed):
    lane = lax.broadcasted_iota(jnp.int32, x.shape, 1)
    partner = jnp.where((lane % 64) < 32,
                        pltpu.roll(x, HEAD_DIM - 32, 1), pltpu.roll(x, 32, 1))
    return x * cos + partner * sin_signed


def _per_sequence(mixer, row_args, seq, n_seq):
    def run(*refs, tile):
        for i in range(n_seq):
            views = [r.at[i * seq:(i + 1) * seq] if k in row_args else r
                     for k, r in enumerate(refs)]
            yield from mixer(*views, tile=tile)
    return run


def _ctx_even_mixer(sink_ref, p_ref, dftc_ref, dftt_ref, o_ref, *, tile):
    del tile
    scores = []
    for hq in range(N_Q_HEADS):
        h = hq // Q_PER_KV
        k = p_ref[:, K_OFF + h * HEAD_DIM:K_OFF + (h + 1) * HEAD_DIM]
        scores.append(_dot_nt(p_ref[:, hq * HEAD_DIM:(hq + 1) * HEAD_DIM], k))
    top, bot = _fourier_cols(p_ref, UB_OFF, dftc_ref[...])
    f = _dot(dftt_ref[...], jnp.concatenate([top, bot], axis=0))
    yield
    for hq in range(N_Q_HEADS):
        h = hq // Q_PER_KV
        v = p_ref[:, V_OFF + h * HEAD_DIM:V_OFF + (h + 1) * HEAD_DIM]
        o = _softmax_pv([scores[hq]], sink_ref[hq], [v])
        ga = p_ref[:, GA_OFF + hq * HEAD_DIM:GA_OFF + (hq + 1) * HEAD_DIM].astype(F32)
        o_ref[:, hq * HEAD_DIM:(hq + 1) * HEAD_DIM] = (_silu(ga) * o).astype(BF16)
        yield
    gb = p_ref[:, GB_OFF:GB_OFF + FOURIER_WIDTH].astype(F32)
    o_ref[:, ATTN_WIDTH:] = (_silu(gb) * f).astype(BF16)
    yield


def _ctx_even(proj, sink, dftc, dftt, seq, **tail):
    m = proj.shape[0]
    rows = CTX_SEQS_PER_STEP * seq
    return _mixer_call(
        _per_sequence(_ctx_even_mixer, (1, 4), seq, CTX_SEQS_PER_STEP), m // rows,
        [pl.BlockSpec(memory_space=pltpu.SMEM),
         pl.BlockSpec((rows, IN_WIDTH), lambda b: (b, 0)),
         pl.BlockSpec(dftc.shape, lambda b: (0, 0)),
         pl.BlockSpec(dftt.shape, lambda b: (0, 0))],
        [sink, proj, dftc, dftt], [],
        rows=rows, mod_row=lambda b: 0, name="context_attention_fourier", **tail)


def _lat_even_mixer(sink_ref, row_ref, k_ref, v_ref, kc_ref, vc_ref, cos_ref, sin_ref,
                    u_ref, dftc_ref, dftt_ref, o_ref, kr_ref, z_ref, *, tile, seq):
    i = tile % (seq // BLOCK)
    rope_rows = 256

    @pl.when(i == 0)
    def _():
        def body(c, carry):
            r = pl.multiple_of(c * rope_rows, rope_rows)
            cos = cos_ref[pl.ds(r, rope_rows), :]
            sin = sin_ref[pl.ds(r, rope_rows), :]
            for h in range(N_KV_HEADS):
                cols = slice(h * HEAD_DIM, (h + 1) * HEAD_DIM)
                kr_ref[pl.ds(r, rope_rows), cols] = _rope(
                    k_ref[pl.ds(r, rope_rows), cols].astype(F32), cos, sin).astype(BF16)
            return carry
        lax.fori_loop(0, seq // rope_rows, body, 0)
        top, bot = _fourier_cols(u_ref, 0, dftc_ref[...])
        z_ref[0:seq, :] = top
        z_ref[seq:2 * seq, :] = bot

    span = 3 * BLOCK
    q0 = pl.multiple_of(i * BLOCK, BLOCK)
    start = pl.multiple_of(jnp.clip(i * BLOCK - BLOCK, 0, seq - span), BLOCK)
    cos = cos_ref[pl.ds(q0, BLOCK), :]
    sin = sin_ref[pl.ds(q0, BLOCK), :]
    dist = ((q0 - start) + lax.broadcasted_iota(jnp.int32, (BLOCK, span), 0)
            - lax.broadcasted_iota(jnp.int32, (BLOCK, span), 1))
    valid = (dist >= -WINDOW) & (dist <= WINDOW)
    valid = jnp.concatenate([valid] * Q_PER_KV, axis=0)
    scores = []
    for h in range(N_KV_HEADS):
        kv_cols = slice(h * HEAD_DIM, (h + 1) * HEAD_DIM)
        q = jnp.concatenate(
            [_rope(row_ref[:, (h * Q_PER_KV + g) * HEAD_DIM:(h * Q_PER_KV + g + 1) * HEAD_DIM]
                   .astype(F32), cos, sin).astype(BF16) for g in range(Q_PER_KV)], axis=0)
        s_loc = _dot_nt(q, kr_ref[pl.ds(start, span), kv_cols])
        scores.append((jnp.where(valid, s_loc, NEG_INF), _dot_nt(q, kc_ref[:, kv_cols])))
    f = _dot(dftt_ref[...], z_ref[...])
    yield
    for h in range(N_KV_HEADS):
        kv_cols = slice(h * HEAD_DIM, (h + 1) * HEAD_DIM)
        sink = jnp.concatenate(
            [jnp.full((BLOCK, 1), sink_ref[h * Q_PER_KV + g], F32) for g in range(Q_PER_KV)],
            axis=0)
        o = _softmax_pv(list(scores[h]), sink,
                        [v_ref[pl.ds(start, span), kv_cols], vc_ref[:, kv_cols]])
        yield
        for g in range(Q_PER_KV):
            hq = h * Q_PER_KV + g
            ga = row_ref[:, GA_OFF + hq * HEAD_DIM:GA_OFF + (hq + 1) * HEAD_DIM].astype(F32)
            o_ref[:, hq * HEAD_DIM:(hq + 1) * HEAD_DIM] = (
                _silu(ga) * o[g * BLOCK:(g + 1) * BLOCK]).astype(BF16)
        yield
    gb = row_ref[:, GB_OFF:GB_OFF + FOURIER_WIDTH].astype(F32)
    o_ref[:, ATTN_WIDTH:] = (_silu(gb) * f).astype(BF16)
    yield


def _lat_even(proj, sink, kc, vc, cos, sin, dftc, dftt, seq, **tail):
    m = proj.shape[0]
    nb = seq // BLOCK
    return _mixer_call(
        functools.partial(_lat_even_mixer, seq=seq), m // BLOCK,
        [pl.BlockSpec(memory_space=pltpu.SMEM),
         pl.BlockSpec((BLOCK, IN_WIDTH), lambda t: (t, 0)),
         pl.BlockSpec((seq, KV_WIDTH), lambda t: (t // nb, K_OFF // KV_WIDTH)),
         pl.BlockSpec((seq, KV_WIDTH), lambda t: (t // nb, V_OFF // KV_WIDTH)),
         pl.BlockSpec((None,) + kc.shape[1:], lambda t: (t // nb, 0, 0)),
         pl.BlockSpec((None,) + vc.shape[1:], lambda t: (t // nb, 0, 0)),
         pl.BlockSpec(cos.shape, lambda t: (0, 0)),
         pl.BlockSpec(sin.shape, lambda t: (0, 0)),
         pl.BlockSpec((seq, FOURIER_WIDTH), lambda t: (t // nb, UB_OFF // FOURIER_WIDTH)),
         pl.BlockSpec(dftc.shape, lambda t: (0, 0)),
         pl.BlockSpec((BLOCK, 2 * seq), lambda t: (t % nb, 0))],
        [sink, proj, proj, proj, kc, vc, cos, sin, proj, dftc, dftt],
        [pltpu.VMEM((seq, KV_WIDTH), BF16), pltpu.VMEM((2 * seq, FOURIER_WIDTH), BF16)],
        rows=BLOCK, mod_row=lambda t: t // nb, name="latent_attention_fourier", **tail)


def _odd_mixer_body(uc_ref, gc_ref, ud_ref, vd_ref, gd_ref, pw_ref, ps_ref, sn_ref, sw_ref,
                    sb_ref, o_ref, *, tile, seq):
    r = tile % (seq // SEQ_TILE)
    span = min(2 * SEQ_TILE, seq)
    t0 = pl.multiple_of(r * SEQ_TILE, SEQ_TILE)
    start = pl.multiple_of(jnp.clip(r * SEQ_TILE - SEQ_TILE // 2, 0, seq - span), SEQ_TILE // 2)
    rel = (lax.broadcasted_iota(jnp.int32, (SEQ_TILE, span), 1) + (start - t0)
           - lax.broadcasted_iota(jnp.int32, (SEQ_TILE, span), 0))
    pos = t0 + lax.broadcasted_iota(jnp.int32, (SEQ_TILE, 1), 0)
    for g, w in enumerate(POOL_WINDOWS):
        cols = slice(g * POOL_GROUP_DIM, (g + 1) * POOL_GROUP_DIM)
        band = jnp.where((rel >= -(w // 2)) & (rel < w // 2), 1.0, 0.0).astype(BF16)
        total = _dot(band, uc_ref[pl.ds(start, span), cols])
        cnt = (jnp.minimum(pos + w // 2, seq) - jnp.maximum(pos - w // 2, 0)).astype(F32)
        pooled = total / cnt - uc_ref[pl.ds(t0, SEQ_TILE), cols].astype(F32)
        mixed = _dot(pooled.astype(BF16), pw_ref[g]) * ps_ref[:, cols]
        o_ref[:, cols] = (_silu(gc_ref[:, cols].astype(F32)) * mixed).astype(BF16)
        yield
    for h in range(SGU_HEADS):
        cols = slice(h * SGU_GROUP_DIM, (h + 1) * SGU_GROUP_DIM)
        v = vd_ref[:, cols].astype(F32)
        mu = jnp.mean(v, axis=-1, keepdims=True)
        vc = v - mu
        var = jnp.mean(vc * vc, axis=-1, keepdims=True)
        vn = (vc * lax.rsqrt(var + EPS) * sn_ref[:, cols]).astype(BF16)
        bias = sb_ref[:, h:h + 1]
        for c in range(SEQ_TILE // SGU_CHUNK):
            rows = slice(c * SGU_CHUNK, (c + 1) * SGU_CHUNK)
            gated = ud_ref[rows, cols].astype(F32) * (_dot(sw_ref[h], vn[rows]) + bias)
            o_ref[rows, POOL_WIDTH + h * SGU_GROUP_DIM:POOL_WIDTH + (h + 1) * SGU_GROUP_DIM] = (
                _silu(gd_ref[rows, cols].astype(F32)) * gated).astype(BF16)
        yield


def _odd_mixer(proj, pool_w_bf, pool_scale, sgu_norm, sgu_w_bf, sgu_b_t, j, seq, n_mod, **tail):
    m = proj.shape[0]
    body = functools.partial(_odd_mixer_body, seq=seq)
    rows = SEQ_TILE
    if seq == SEQ_TILE:
        rows = CTX_SEQS_PER_STEP * seq
        body = _per_sequence(body, (0, 1, 2, 3, 4, 10), seq, CTX_SEQS_PER_STEP)
    uc_rows = max(rows, seq)
    steps_per_seq = uc_rows // rows

    def tile_spec(off):
        return pl.BlockSpec((rows, POOL_WIDTH), lambda t: (t, off // POOL_WIDTH))

    return _mixer_call(
        body, m // rows,
        [pl.BlockSpec((uc_rows, POOL_WIDTH),
                      lambda t: (t // steps_per_seq, UC_OFF // POOL_WIDTH)),
         tile_spec(GC_OFF), tile_spec(UD_OFF), tile_spec(VD_OFF), tile_spec(GD_OFF),
         pl.BlockSpec((None,) + pool_w_bf.shape[1:], lambda t: (j, 0, 0, 0)),
         pl.BlockSpec((None, 1, POOL_WIDTH), lambda t: (j, 0, 0)),
         pl.BlockSpec((None, 1, SGU_WIDTH), lambda t: (j, 0, 0)),
         pl.BlockSpec((None,) + sgu_w_bf.shape[1:], lambda t: (j, 0, 0, 0)),
         pl.BlockSpec((None,) + sgu_b_t.shape[1:], lambda t: (j, 0, 0))],
        [proj, proj, proj, proj, proj, pool_w_bf, pool_scale.reshape(-1, 1, POOL_WIDTH),
         sgu_norm.reshape(-1, 1, SGU_WIDTH), sgu_w_bf, sgu_b_t], [],
        rows=rows, mod_row=(lambda t: t // steps_per_seq) if n_mod > 1 else (lambda t: 0),
        name="pool_spatial_gating", **tail)


def _dft_tables(seq):
    def cos_sin(n):
        idx = np.arange(n)
        ang = 2.0 * np.pi * ((idx[:, None] * idx[None, :]) % n) / n
        return np.cos(ang) / np.sqrt(n), np.sin(ang) / np.sqrt(n)

    cc, sc = cos_sin(FOURIER_GROUP_DIM)
    ct, st = cos_sin(seq)
    dftc = jnp.asarray(np.concatenate([cc, sc], axis=1), dtype=F32).astype(BF16)
    dftt = jnp.asarray(np.concatenate([ct, -st], axis=1), dtype=F32).astype(BF16)
    return dftc, dftt


def _rope_tables(seq):
    n_freq = HEAD_DIM // 4
    t = np.arange(seq)
    inv = ROPE_THETA ** (-np.arange(n_freq, dtype=np.float64) / n_freq)
    ang = np.stack([(t // GRID_W)[:, None] * inv, (t % GRID_W)[:, None] * inv], axis=1)
    cos = np.repeat(np.cos(ang)[:, :, None, :], 2, axis=2).reshape(seq, HEAD_DIM)
    sin = np.sin(ang)
    sin = np.stack([-sin, sin], axis=2).reshape(seq, HEAD_DIM)
    return jnp.asarray(cos, dtype=F32), jnp.asarray(sin, dtype=F32)


def kernel(x_prompt, x_sample, cache_k, cache_v, c, c_ctx, w_ada, b_ada, norm_pre, norm_post,
           w_in, w_out, attn_sink, pool_w, pool_scale, sgu_norm, sgu_w, sgu_b):
    batch, seq, _ = x_prompt.shape
    dec_batch, dec_seq, _ = x_sample.shape
    n_attn, past = cache_k.shape[1], cache_k.shape[2]

    pool_w_bf = pool_w.astype(BF16)
    sgu_w_bf = sgu_w.astype(BF16)
    sgu_b_t = jnp.swapaxes(sgu_b, 1, 2)
    kc = cache_k.reshape(dec_batch, n_attn, past, KV_WIDTH).astype(BF16)
    vc = cache_v.reshape(dec_batch, n_attn, past, KV_WIDTH).astype(BF16)

    conds = jnp.concatenate(
        [c_ctx[None, :], c, jnp.zeros((COND_ROWS - 1 - dec_batch, D_MODEL), F32)], axis=0)
    mod = _ada(conds, w_ada, b_ada)
    mod_p = mod[:, 0:1].reshape(DEPTH, 1, 1, 3 * D_MODEL)
    mod_s = mod[:, 1:1 + dec_batch].reshape(DEPTH, dec_batch, 1, 3 * D_MODEL)

    dftc, dftt_p = _dft_tables(seq)
    _, dftt_s = _dft_tables(dec_seq)
    cos, sin = _rope_tables(dec_seq)

    xp = x_prompt.reshape(batch * seq, D_MODEL)
    xs = x_sample.reshape(dec_batch * dec_seq, D_MODEL)
    cache = None
    cache_shape = (batch, n_attn, seq * N_KV_HEADS, HEAD_DIM)
    w_in_bf = w_in[0].astype(BF16)
    for l in range(DEPTH):
        j = l // 2
        q_scale = ATTN_SCALE if l % 2 == 0 else None
        last = l == DEPTH - 1
        res_p = _inproj(xp, mod_p, norm_pre, w_in_bf, l, q_scale=q_scale, cache=cache,
                        cache_shape=cache_shape if l % 2 == 0 else None,
                        cast=None if last else (w_in, l + 1))
        proj_s, w_out_bf = _inproj(xs, mod_s, norm_pre, w_in_bf, l, q_scale=q_scale,
                                   cast=(w_out, l))
        if not last:
            *res_p, w_in_bf = res_p
        tail_p = dict(x=xp, mod=mod_p, norm_post=norm_post, w_out_bf=w_out_bf, layer=l)
        tail_s = dict(x=xs, mod=mod_s, norm_post=norm_post, w_out_bf=w_out_bf, layer=l)
        if l % 2 == 0:
            proj_p, *cache = res_p
            xp = _ctx_even(proj_p, attn_sink[j], dftc, dftt_p, seq, **tail_p)
            xs = _lat_even(proj_s, attn_sink[j], kc[:, j], vc[:, j], cos, sin, dftc, dftt_s,
                           dec_seq, **tail_s)
        else:
            proj_p, = res_p
            odd = (pool_w_bf, pool_scale, sgu_norm, sgu_w_bf, sgu_b_t, j)
            xp = _odd_mixer(proj_p, *odd, seq, 1, **tail_p)
            xs = _odd_mixer(proj_s, *odd, dec_seq, dec_batch, **tail_s)

    new_k, new_v = (t.reshape(batch, n_attn, seq, N_KV_HEADS, HEAD_DIM) for t in cache)
    return (xp.reshape(batch, seq, D_MODEL), xs.reshape(dec_batch, dec_seq, D_MODEL), new_k, new_v)
```
